```python
import jax, jax.numpy as jnp
from jax import lax
import numpy as np

D_MODEL = 1024
BATCH = 8
SEQ = 2048
DEPTH = 2

CHUNK = 64
HEAD_DIM = 64
N_HEADS_SB = 4
N_HEADS_CH = 8
N_HEADS_FOX = 4
W_SB = N_HEADS_SB * HEAD_DIM
W_CH = N_HEADS_CH * HEAD_DIM
W_FOX = N_HEADS_FOX * HEAD_DIM
LEFT_CHUNKS = 8
BAND = (LEFT_CHUNKS + 1) * CHUNK
MAX_REL = 128
N_REL = 2 * MAX_REL + 1
Q_BLOCK = 128
N_BRANCH = 3
D_FF = ((8 * D_MODEL // 3 + 127) // 128) * 128
QKV_WIDTH = 3 * (W_SB + W_CH + W_FOX)
FORGET_OFFSET = QKV_WIDTH
IN_WIDTH = QKV_WIDTH + N_HEADS_FOX + N_BRANCH * D_MODEL
SPLIT_SIZES = (W_SB, W_SB, W_SB, W_CH, W_CH, W_CH, W_FOX, W_FOX, W_FOX, N_HEADS_FOX, D_MODEL, D_MODEL, D_MODEL)
SPLIT_POINTS = tuple(int(v) for v in np.cumsum(SPLIT_SIZES)[:-1])
FORGET_BIAS_INIT = 4.0
RMS_EPS = 1e-6
NEG = -1e30

kernel_name = "hybrid_stickbreak_chunkrel_fox_macaron"


def rmsnorm(x, g):
    xf = x.astype(jnp.float32)
    y = xf * lax.rsqrt(jnp.mean(xf * xf, axis=-1, keepdims=True) + RMS_EPS)
    return (y * g.astype(jnp.float32)).astype(x.dtype)


def swiglu(h, w_in, w_out):
    gate, up = jnp.split(h @ w_in, 2, axis=-1)
    return (jax.nn.silu(gate) * up) @ w_out


def split_heads(t, n_heads):
    b, s, _ = t.shape
    return t.reshape(b, s, n_heads, HEAD_DIM).transpose(0, 2, 1, 3)


def merge_heads(t):
    b, h, s, d = t.shape
    return t.transpose(0, 2, 1, 3).reshape(b, s, h * d)


def stick_breaking_attention(q, k, v):
    T = q.shape[2]
    scale = HEAD_DIM ** -0.5
    outs = []
    for start in range(0, T, Q_BLOCK):
        end = start + Q_BLOCK
        z = jnp.einsum('bhqd,bhkd->bhqk', q[:, :, start:end], k[:, :, :end]).astype(jnp.float32) * scale
        strict = jnp.arange(end)[None, :] < jnp.arange(start, end)[:, None]
        log_beta = jax.nn.log_sigmoid(z)
        log_fail = jnp.where(strict, jax.nn.log_sigmoid(-z), 0.0)
        between = lax.cumsum(log_fail, axis=3, reverse=True) - log_fail
        w = jnp.where(strict, jnp.exp(log_beta + between), 0.0)
        outs.append(jnp.einsum('bhqk,bhkd->bhqd', w.astype(v.dtype), v[:, :, :end]))
    return jnp.concatenate(outs, axis=2)


def forgetting_attention(q, k, v, log_f):
    T = q.shape[2]
    scale = HEAD_DIM ** -0.5
    F = jnp.cumsum(log_f, axis=-1)
    outs = []
    for start in range(0, T, Q_BLOCK):
        end = start + Q_BLOCK
        z = jnp.einsum('bhqd,bhkd->bhqk', q[:, :, start:end], k[:, :, :end]).astype(jnp.float32) * scale
        z = z + F[:, :, start:end, None] - F[:, :, None, :end]
        causal = jnp.arange(end)[None, :] <= jnp.arange(start, end)[:, None]
        p = jax.nn.softmax(jnp.where(causal, z, NEG), axis=-1)
        outs.append(jnp.einsum('bhqk,bhkd->bhqd', p.astype(v.dtype), v[:, :, :end]))
    return jnp.concatenate(outs, axis=2)


def chunked_relpos_attention(q, k, v, rel_table):
    B, H, T, Dh = q.shape
    nc = T // CHUNK
    scale = Dh ** -0.5
    qc = q.reshape(B, H, nc, CHUNK, Dh)
    pad = ((0, 0), (0, 0), (LEFT_CHUNKS * CHUNK, 0), (0, 0))
    kp = jnp.pad(k, pad).reshape(B, H, nc + LEFT_CHUNKS, CHUNK, Dh)
    vp = jnp.pad(v, pad).reshape(B, H, nc + LEFT_CHUNKS, CHUNK, Dh)
    band_idx = jnp.arange(nc)[:, None] + jnp.arange(LEFT_CHUNKS + 1)[None, :]
    k_band = kp[:, :, band_idx].reshape(B, H, nc, BAND, Dh)
    v_band = vp[:, :, band_idx].reshape(B, H, nc, BAND, Dh)
    z = jnp.einsum('bhcqd,bhckd->bhcqk', qc, k_band).astype(jnp.float32) * scale
    rel = (jnp.arange(CHUNK)[:, None] + LEFT_CHUNKS * CHUNK) - jnp.arange(BAND)[None, :]
    rel = jnp.clip(rel, -MAX_REL, MAX_REL) + MAX_REL
    bias = rel_table[rel].astype(jnp.float32).transpose(2, 0, 1)
    z = z + bias[None, :, None]
    key_abs = (jnp.arange(nc)[:, None] - LEFT_CHUNKS) * CHUNK + jnp.arange(BAND)[None, :]
    valid = key_abs >= 0
    p = jax.nn.softmax(jnp.where(valid[None, None, :, None, :], z, NEG), axis=-1)
    o = jnp.einsum('bhcqk,bhckd->bhcqd', p.astype(v.dtype), v_band)
    return o.reshape(B, H, T, Dh)


def hybrid_layer(x, g_ffn1, w_ffn1_in, w_ffn1_out, g_mix, w_in, b_in, rel_bias,
                 w_br_sb, w_br_ch, w_br_fox, w_out, g_ffn2, w_ffn2_in, w_ffn2_out):
    x = x + 0.5 * swiglu(rmsnorm(x, g_ffn1), w_ffn1_in, w_ffn1_out)
    h = rmsnorm(x, g_mix)
    proj = h @ w_in + b_in
    (q_a, k_a, v_a, q_b, k_b, v_b, q_c, k_c, v_c,
     f_logit, g_a, g_b, g_c) = jnp.split(proj, list(SPLIT_POINTS), axis=-1)
    o_a = stick_breaking_attention(split_heads(q_a, N_HEADS_SB), split_heads(k_a, N_HEADS_SB),
                                   split_heads(v_a, N_HEADS_SB))
    o_b = chunked_relpos_attention(split_heads(q_b, N_HEADS_CH), split_heads(k_b, N_HEADS_CH),
                                   split_heads(v_b, N_HEADS_CH), rel_bias)
    log_f = jax.nn.log_sigmoid(f_logit.astype(jnp.float32)).transpose(0, 2, 1)
    o_c = forgetting_attention(split_heads(q_c, N_HEADS_FOX), split_heads(k_c, N_HEADS_FOX),
                               split_heads(v_c, N_HEADS_FOX), log_f)
    merged = (jax.nn.sigmoid(g_a) * (merge_heads(o_a) @ w_br_sb)
              + jax.nn.sigmoid(g_b) * (merge_heads(o_b) @ w_br_ch)
              + jax.nn.sigmoid(g_c) * (merge_heads(o_c) @ w_br_fox))
    x = x + merged @ w_out
    x = x + 0.5 * swiglu(rmsnorm(x, g_ffn2), w_ffn2_in, w_ffn2_out)
    return x


def setup_inputs(seed: int = 0) -> dict:
    key = jax.random.key(seed)
    ks = jax.random.split(key, 18)

    def dense(k, shape, fan_in):
        return jax.random.normal(k, shape, jnp.float32) * fan_in ** -0.5

    def gain(k, shape):
        return 1.0 + 0.05 * jax.random.normal(k, shape, jnp.float32)

    b_in = 0.02 * jax.random.normal(ks[6], (DEPTH, IN_WIDTH), jnp.float32)
    b_in = b_in.at[:, FORGET_OFFSET:FORGET_OFFSET + N_HEADS_FOX].add(FORGET_BIAS_INIT)
    return {
        "x": jax.random.normal(ks[0], (BATCH, SEQ, D_MODEL), jnp.float32),
        "g_ffn1": gain(ks[1], (DEPTH, D_MODEL)),
        "w_ffn1_in": dense(ks[2], (DEPTH, D_MODEL, 2 * D_FF), D_MODEL),
        "w_ffn1_out": dense(ks[3], (DEPTH, D_FF, D_MODEL), D_FF),
        "g_mix": gain(ks[4], (DEPTH, D_MODEL)),
        "w_in": dense(ks[5], (DEPTH, D_MODEL, IN_WIDTH), D_MODEL),
        "b_in": b_in,
        "rel_bias": 0.1 * jax.random.normal(ks[7], (DEPTH, N_REL, N_HEADS_CH), jnp.float32),
        "w_br_sb": dense(ks[8], (DEPTH, W_SB, D_MODEL), W_SB),
        "w_br_ch": dense(ks[9], (DEPTH, W_CH, D_MODEL), W_CH),
        "w_br_fox": dense(ks[10], (DEPTH, W_FOX, D_MODEL), W_FOX),
        "w_out": dense(ks[11], (DEPTH, D_MODEL, D_MODEL), D_MODEL),
        "g_ffn2": gain(ks[12], (DEPTH, D_MODEL)),
        "w_ffn2_in": dense(ks[13], (DEPTH, D_MODEL, 2 * D_FF), D_MODEL),
        "w_ffn2_out": dense(ks[14], (DEPTH, D_FF, D_MODEL), D_FF),
        "g_final": gain(ks[15], (D_MODEL,)),
    }


def reference(x, g_ffn1, w_ffn1_in, w_ffn1_out, g_mix, w_in, b_in, rel_bias,
              w_br_sb, w_br_ch, w_br_fox, w_out, g_ffn2, w_ffn2_in, w_ffn2_out, g_final):
    for layer in range(DEPTH):
        x = hybrid_layer(x, g_ffn1[layer], w_ffn1_in[layer], w_ffn1_out[layer], g_mix[layer],
                         w_in[layer], b_in[layer], rel_bias[layer], w_br_sb[layer], w_br_ch[layer],
                         w_br_fox[layer], w_out[layer], g_ffn2[layer], w_ffn2_in[layer],
                         w_ffn2_out[layer])
    return rmsnorm(x, g_final)
```

```python
import functools

import numpy as np
import jax
import jax.numpy as jnp
from jax import lax
from jax.experimental import pallas as pl
from jax.experimental.pallas import tpu as pltpu

D_MODEL = 1024
HEAD_DIM = 64
CHUNK = 64
LEFT_CHUNKS = 8
MAX_REL = 128
N_HEADS_SB = 4
N_HEADS_CH = 8
N_HEADS_FOX = 4
W_SB = N_HEADS_SB * HEAD_DIM
W_CH = N_HEADS_CH * HEAD_DIM
W_FOX = N_HEADS_FOX * HEAD_DIM
QKV_WIDTH = 3 * (W_SB + W_CH + W_FOX)
RMS_EPS = 1e-6
NEG = -1e30
LOG2E = 1.4426950408889634

LANES = 128
PAIR_Q = 2 * CHUNK
PAIR_KEYS = PAIR_Q + LEFT_CHUNKS * CHUNK
REL_VEC = 768
VMEM_LIMIT = 56 * 1024 * 1024

F32 = jnp.float32
BF16 = jnp.bfloat16


def _rms_bf16(x, g):
    ms = jnp.mean(x * x, axis=-1, keepdims=True)
    return ((x * lax.rsqrt(ms + RMS_EPS)) * g).astype(BF16)


def _dot(a, b):
    return jnp.dot(a, b, preferred_element_type=F32)


def _dot_nt(a, b):
    return lax.dot_general(a, b, (((1,), (1,)), ((), ())), preferred_element_type=F32)


def _const_spec(shape):
    nd = len(shape)
    return pl.BlockSpec(shape, lambda *_: (0,) * nd, pipeline_mode=pl.Buffered(1))


def _params(sem):
    return pltpu.CompilerParams(dimension_semantics=sem, vmem_limit_bytes=VMEM_LIMIT)


def _ffn_kernel(x_ref, g_ref, wg_ref, wu_ref, wo_ref, gf_ref, o_ref, *, final_norm):
    x = x_ref[...]
    h = _rms_bf16(x, g_ref[...])
    gate = _dot(h, wg_ref[...])
    up = _dot(h, wu_ref[...])
    a = (gate * jax.nn.sigmoid(gate) * up).astype(BF16)
    y = x + 0.5 * _dot(a, wo_ref[...])
    if final_norm:
        ms = jnp.mean(y * y, axis=-1, keepdims=True)
        y = (y * lax.rsqrt(ms + RMS_EPS)) * gf_ref[...]
    o_ref[...] = y


def _ffn(x, g, wg, wu, wo, gf, *, final_norm, tm=512):
    n, d = x.shape
    row = pl.BlockSpec((tm, d), lambda i: (i, 0))
    return pl.pallas_call(
        functools.partial(_ffn_kernel, final_norm=final_norm),
        grid=(n // tm,),
        in_specs=[row, _const_spec(g.shape), _const_spec(wg.shape), _const_spec(wu.shape),
                  _const_spec(wo.shape), _const_spec(gf.shape)],
        out_specs=row,
        out_shape=jax.ShapeDtypeStruct((n, d), F32),
        compiler_params=_params(("parallel",)),
        name="ffn",
    )(x, g, wg, wu, wo, gf)


def _inproj_kernel(x_ref, g_ref, w_ref, b_ref, wf_ref, bf_ref, qkv_ref, fc_ref, carry_ref, *, tm):
    t = pl.program_id(1)
    h = _rms_bf16(x_ref[...], g_ref[...])
    qkv_ref[...] = (_dot(h, w_ref[...]) + b_ref[...]).astype(BF16)
    f = _dot(h, wf_ref[...]) + bf_ref[...]
    lf = (jnp.minimum(f, 0.0) - jnp.log1p(jnp.exp(-jnp.abs(f)))) * LOG2E
    s = lf.T[:8, :]
    lane = lax.broadcasted_iota(jnp.int32, s.shape, 1)
    sh = 1
    while sh < tm:
        s = s + jnp.where(lane >= sh, pltpu.roll(s, sh, axis=1), 0.0)
        sh *= 2

    @pl.when(t == 0)
    def _():
        carry_ref[...] = jnp.zeros_like(carry_ref)

    fc = s + carry_ref[:, 0:1]
    fc_ref[0] = fc
    carry_ref[...] = jnp.broadcast_to(fc[:, tm - 1:tm], carry_ref.shape)


def _inproj(x, g, w, b, wf, bfg, *, batch, tm=512):
    n, d = x.shape
    seq = n // batch
    nt = seq // tm
    return pl.pallas_call(
        functools.partial(_inproj_kernel, tm=tm),
        grid=(batch, nt),
        in_specs=[pl.BlockSpec((tm, d), lambda bi, ti: (bi * nt + ti, 0)),
                  _const_spec(g.shape), _const_spec(w.shape), _const_spec(b.shape),
                  _const_spec(wf.shape), _const_spec(bfg.shape)],
        out_specs=[pl.BlockSpec((tm, QKV_WIDTH), lambda bi, ti: (bi * nt + ti, 0)),
                   pl.BlockSpec((1, 8, tm), lambda bi, ti: (bi, 0, ti))],
        out_shape=[jax.ShapeDtypeStruct((n, QKV_WIDTH), BF16),
                   jax.ShapeDtypeStruct((batch, 8, seq), F32)],
        scratch_shapes=[pltpu.VMEM((8, LANES), F32)],
        compiler_params=_params(("parallel", "arbitrary")),
        name="inproj",
    )(x, g, w, b, wf, bfg)


def _head_masks(shape):
    lane = lax.broadcasted_iota(jnp.int32, shape, 1)
    lo = lane < HEAD_DIM
    return lo, jnp.logical_not(lo)


def _attn_specs(seq, tq, col_q, col_k, col_v):
    q = pl.BlockSpec((1, tq, LANES), lambda b, p, i: (b, i, col_q + p))
    k = pl.BlockSpec((1, seq, LANES), lambda b, p, i: (b, 0, col_k + p))
    v = pl.BlockSpec((1, seq, LANES), lambda b, p, i: (b, 0, col_v + p))
    return q, k, v


def _sb_kernel(q_ref, k_ref, v_ref, o_ref, *, tq):
    i = pl.program_id(2)
    q = q_ref[0]
    row = lax.broadcasted_iota(jnp.int32, (tq, tq), 0)
    col = lax.broadcasted_iota(jnp.int32, (tq, tq), 1)
    strict = col < row
    tri = jnp.where(row > col, 1.0, 0.0).astype(BF16)
    masks = _head_masks((tq, LANES))

    def block(kb, carry, acc, qm, masked):
        ks = pl.multiple_of(kb * tq, tq)
        k = k_ref[0, pl.ds(ks, tq), :]
        v = v_ref[0, pl.ds(ks, tq), :]
        z = _dot_nt(qm, k)
        sp = jnp.maximum(z, 0.0) + jnp.log2(1.0 + jnp.exp2(-jnp.abs(z)))
        lf = -sp
        if masked:
            lf = jnp.where(strict, lf, 0.0)
        hi = lf.astype(BF16)
        lo = (lf - hi.astype(F32)).astype(BF16)
        between = _dot(hi, tri) + _dot(lo, tri) + carry
        w = jnp.exp2(z - sp + between)
        if masked:
            w = jnp.where(strict, w, 0.0)
        acc = acc + _dot(w.astype(BF16), v)
        carry = carry + jnp.sum(lf, axis=1, keepdims=True)
        return carry, acc

    outs = []
    for e in range(2):
        qm = jnp.where(masks[e], q, jnp.zeros_like(q))
        carry, acc = block(i, jnp.zeros((tq, 1), F32), jnp.zeros((tq, LANES), F32), qm, True)

        def body(n, c, qm=qm):
            return block(i - 1 - n, c[0], c[1], qm, False)

        carry, acc = lax.fori_loop(0, i, body, (carry, acc))
        outs.append(acc)
    o_ref[0] = jnp.where(masks[0], outs[0], outs[1]).astype(BF16)


def _sb_attention(qkv, *, tq=256):
    batch, seq, _ = qkv.shape
    cb = 0
    q, k, v = _attn_specs(seq, tq, cb, cb + W_SB // LANES, cb + 2 * W_SB // LANES)
    return pl.pallas_call(
        functools.partial(_sb_kernel, tq=tq),
        grid=(batch, W_SB // LANES, seq // tq),
        in_specs=[q, k, v],
        out_specs=pl.BlockSpec((1, tq, LANES), lambda b, p, i: (b, i, p)),
        out_shape=jax.ShapeDtypeStruct((batch, seq, W_SB), BF16),
        compiler_params=_params(("parallel", "parallel", "arbitrary")),
        name="sb_attn",
    )(qkv, qkv, qkv)


def _fox_kernel(q_ref, k_ref, v_ref, f_ref, o_ref, *, tq):
    i = pl.program_id(2)
    q = q_ref[0]
    row = lax.broadcasted_iota(jnp.int32, (tq, tq), 0)
    col = lax.broadcasted_iota(jnp.int32, (tq, tq), 1)
    causal = col <= row
    masks = _head_masks((tq, LANES))
    q0 = pl.multiple_of(i * tq, tq)

    def block(kb, m, l, acc, qm, e, fq0, masked):
        ks = pl.multiple_of(kb * tq, tq)
        k = k_ref[0, pl.ds(ks, tq), :]
        v = v_ref[0, pl.ds(ks, tq), :]
        s = _dot_nt(qm, k) + (fq0 - f_ref[e, :, pl.ds(ks, tq)])
        if masked:
            s = jnp.where(causal, s, NEG)
        m_new = jnp.maximum(m, jnp.max(s, axis=1, keepdims=True))
        alpha = jnp.exp2(m - m_new)
        p = jnp.exp2(s - m_new)
        l = alpha * l + jnp.sum(p, axis=1, keepdims=True)
        acc = alpha * acc + _dot(p.astype(BF16), v)
        return m_new, l, acc

    outs = []
    for e in range(2):
        qm = jnp.where(masks[e], q, jnp.zeros_like(q))
        fq0 = f_ref[e, :, pl.ds(q0, tq)][:, 0:1]
        init = (jnp.full((tq, 1), NEG, F32), jnp.zeros((tq, 1), F32), jnp.zeros((tq, LANES), F32))

        def body(kb, c, qm=qm, e=e, fq0=fq0):
            return block(kb, c[0], c[1], c[2], qm, e, fq0, False)

        m, l, acc = lax.fori_loop(0, i, body, init)
        m, l, acc = block(i, m, l, acc, qm, e, fq0, True)
        outs.append(acc * (1.0 / l))
    o_ref[0] = jnp.where(masks[0], outs[0], outs[1]).astype(BF16)


def _fox_attention(qkv, fcum, *, tq=256):
    batch, seq, _ = qkv.shape
    cb = (3 * W_SB + 3 * W_CH) // LANES
    q, k, v = _attn_specs(seq, tq, cb, cb + W_FOX // LANES, cb + 2 * W_FOX // LANES)
    f = pl.BlockSpec((None, 2, 1, seq), lambda b, p, i: (b, p, 0, 0))
    return pl.pallas_call(
        functools.partial(_fox_kernel, tq=tq),
        grid=(batch, W_FOX // LANES, seq // tq),
        in_specs=[q, k, v, f],
        out_specs=pl.BlockSpec((1, tq, LANES), lambda b, p, i: (b, i, p)),
        out_shape=jax.ShapeDtypeStruct((batch, seq, W_FOX), BF16),
        compiler_params=_params(("parallel", "parallel", "arbitrary")),
        name="fox_attn",
    )(qkv, qkv, qkv, fcum)


def _chunk_kernel(q_ref, k_ref, v_ref, r_ref, o_ref, bias_ref, *, seq):
    n_tiles = seq // PAIR_Q
    n_edge = LEFT_CHUNKS * CHUNK // PAIR_Q
    rows = lax.broadcasted_iota(jnp.int32, (PAIR_Q, PAIR_KEYS), 0)
    cols = lax.broadcasted_iota(jnp.int32, (PAIR_Q, PAIR_KEYS), 1)
    first_key = jnp.where(rows < CHUNK, 0, CHUNK)
    valid = (cols >= first_key) & (cols < first_key + PAIR_KEYS - CHUNK)
    masks = _head_masks((PAIR_Q, LANES))
    for e in range(2):
        gen = jnp.broadcast_to(r_ref[0, e:e + 1, :], (PAIR_Q, REL_VEC))
        toeplitz = pltpu.roll(gen, 0, 1, stride=1, stride_axis=0)
        bias_ref[e] = jnp.where(valid, toeplitz[:, :PAIR_KEYS], NEG)

    def tile(m, k0, nk):
        r0 = pl.multiple_of(m * PAIR_Q, PAIR_Q)
        q = q_ref[0, pl.ds(r0, PAIR_Q), :]
        k = k_ref[0, pl.ds(k0, nk), :]
        v = v_ref[0, pl.ds(k0, nk), :]
        outs = []
        for e in range(2):
            qm = jnp.where(masks[e], q, jnp.zeros_like(q))
            s = _dot_nt(qm, k) + bias_ref[e, :, PAIR_KEYS - nk:]
            p = jnp.exp2(s - jnp.max(s, axis=1, keepdims=True))
            l = jnp.sum(p, axis=1, keepdims=True)
            outs.append(_dot(p.astype(BF16), v) * (1.0 / l))
        o_ref[0, pl.ds(r0, PAIR_Q), :] = jnp.where(masks[0], outs[0], outs[1]).astype(BF16)

    for m in range(n_edge):
        tile(m, 0, PAIR_Q * (m + 1))

    def body(m, c):
        tile(m, pl.multiple_of((m - n_edge) * PAIR_Q, PAIR_Q), PAIR_KEYS)
        return c

    lax.fori_loop(n_edge, n_tiles, body, 0)


def _chunk_attention(qkv, rel_gen):
    batch, seq, _ = qkv.shape
    cb = 3 * W_SB // LANES
    npair = W_CH // LANES
    blk = lambda off: pl.BlockSpec((1, seq, LANES), lambda b, p: (b, 0, cb + off + p))
    return pl.pallas_call(
        functools.partial(_chunk_kernel, seq=seq),
        grid=(batch, npair),
        in_specs=[blk(0), blk(npair), blk(2 * npair),
                  pl.BlockSpec((1, 2, REL_VEC), lambda b, p: (p, 0, 0))],
        out_specs=pl.BlockSpec((1, seq, LANES), lambda b, p: (b, 0, p)),
        out_shape=jax.ShapeDtypeStruct((batch, seq, W_CH), BF16),
        scratch_shapes=[pltpu.VMEM((2, PAIR_Q, PAIR_KEYS), F32)],
        compiler_params=_params(("parallel", "parallel")),
        name="chunk_attn",
    )(qkv, qkv, qkv, rel_gen)


def _merge_kernel(x_ref, oa_ref, ob_ref, oc_ref, g_ref, wg_ref, bg_ref, wa_ref, wb_ref, wc_ref, wo_ref, o_ref):
    x = x_ref[...]
    h = _rms_bf16(x, g_ref[...])
    gates = jax.nn.sigmoid(_dot(h, wg_ref[...]) + bg_ref[...])
    merged = (gates[:, :D_MODEL] * _dot(oa_ref[...], wa_ref[...])
              + gates[:, D_MODEL:2 * D_MODEL] * _dot(ob_ref[...], wb_ref[...])
              + gates[:, 2 * D_MODEL:] * _dot(oc_ref[...], wc_ref[...]))
    o_ref[...] = x + _dot(merged.astype(BF16), wo_ref[...])


def _merge(x, oa, ob, oc, g, wg, bg, wa, wb, wc, wo, *, tm=512):
    n, d = x.shape
    rows = lambda w: pl.BlockSpec((tm, w), lambda i: (i, 0))
    consts = [g, wg, bg, wa, wb, wc, wo]
    return pl.pallas_call(
        _merge_kernel,
        grid=(n // tm,),
        in_specs=[rows(d), rows(oa.shape[1]), rows(ob.shape[1]), rows(oc.shape[1])]
                 + [_const_spec(c.shape) for c in consts],
        out_specs=rows(d),
        out_shape=jax.ShapeDtypeStruct((n, d), F32),
        compiler_params=_params(("parallel",)),
        name="merge",
    )(x, oa, ob, oc, *consts)


def _rel_generator_index():
    j = np.arange(REL_VEC)
    dist = np.where(j <= PAIR_KEYS, LEFT_CHUNKS * CHUNK - j, MAX_REL)
    return np.clip(dist, -MAX_REL, MAX_REL) + MAX_REL


def _prep_layer(w_in, b_in, rel_bias):
    qscale = HEAD_DIM ** -0.5 * LOG2E
    col_scale = np.ones((QKV_WIDTH,), np.float32)
    for start, width in ((0, W_SB), (3 * W_SB, W_CH), (3 * W_SB + 3 * W_CH, W_FOX)):
        col_scale[start:start + width] = qscale
    col_scale = jnp.asarray(col_scale)
    w_qkv = (w_in[:, :QKV_WIDTH] * col_scale).astype(BF16)
    b_qkv = (b_in[:QKV_WIDTH] * col_scale)[None, :]
    nf = N_HEADS_FOX
    w_f = jnp.pad(w_in[:, QKV_WIDTH:QKV_WIDTH + nf], ((0, 0), (0, LANES - nf))).astype(BF16)
    b_f = jnp.pad(b_in[QKV_WIDTH:QKV_WIDTH + nf], (0, LANES - nf))[None, :]
    w_g = w_in[:, QKV_WIDTH + nf:].astype(BF16)
    b_g = b_in[QKV_WIDTH + nf:][None, :]
    rel_gen = (rel_bias[_rel_generator_index(), :] * LOG2E).T.reshape(N_HEADS_CH // 2, 2, REL_VEC)
    return w_qkv, b_qkv, w_f, b_f, w_g, b_g, rel_gen


def kernel(x, g_ffn1, w_ffn1_in, w_ffn1_out, g_mix, w_in, b_in, rel_bias, w_br_sb, w_br_ch, w_br_fox, w_out,
           g_ffn2, w_ffn2_in, w_ffn2_out, g_final):
    batch, seq, d = x.shape
    depth = g_ffn1.shape[0]
    d_ff = w_ffn1_out.shape[1]
    xf = x.reshape(batch * seq, d)
    gf = g_final[None, :]
    for l in range(depth):
        xf = _ffn(xf, g_ffn1[l][None, :], w_ffn1_in[l, :, :d_ff].astype(BF16), w_ffn1_in[l, :, d_ff:].astype(BF16),
                  w_ffn1_out[l].astype(BF16), gf, final_norm=False)
        w_qkv, b_qkv, w_f, b_f, w_g, b_g, rel_gen = _prep_layer(w_in[l], b_in[l], rel_bias[l])
        qkv, fcum = _inproj(xf, g_mix[l][None, :], w_qkv, b_qkv, w_f, b_f, batch=batch)
        qkv = qkv.reshape(batch, seq, QKV_WIDTH)
        o_a = _sb_attention(qkv)
        o_b = _chunk_attention(qkv, rel_gen)
        o_c = _fox_attention(qkv, fcum[:, :N_HEADS_FOX].reshape(batch, N_HEADS_FOX, 1, seq))
        xf = _merge(xf, o_a.reshape(batch * seq, W_SB), o_b.reshape(batch * seq, W_CH),
                    o_c.reshape(batch * seq, W_FOX), g_mix[l][None, :], w_g, b_g,
                    w_br_sb[l].astype(BF16), w_br_ch[l].astype(BF16), w_br_fox[l].astype(BF16),
                    w_out[l].astype(BF16))
        xf = _ffn(xf, g_ffn2[l][None, :], w_ffn2_in[l, :, :d_ff].astype(BF16), w_ffn2_in[l, :, d_ff:].astype(BF16),
                  w_ffn2_out[l].astype(BF16), gf, final_norm=(l == depth - 1))
    return xf.reshape(batch, seq, d)
```

```python
import functools

import numpy as np
import jax
import jax.numpy as jnp
from jax import lax
from jax.experimental import pallas as pl
from jax.experimental.pallas import tpu as pltpu

D_MODEL = 1024
HEAD_DIM = 64
CHUNK = 64
LEFT_CHUNKS = 8
MAX_REL = 128
N_HEADS_SB = 4
N_HEADS_CH = 8
N_HEADS_FOX = 4
W_SB = N_HEADS_SB * HEAD_DIM
W_CH = N_HEADS_CH * HEAD_DIM
W_FOX = N_HEADS_FOX * HEAD_DIM
QKV_WIDTH = 3 * (W_SB + W_CH + W_FOX)
RMS_EPS = 1e-6
NEG = -1e30
LOG2E = 1.4426950408889634

LANES = 128
PAIR_Q = 2 * CHUNK
PAIR_KEYS = PAIR_Q + LEFT_CHUNKS * CHUNK
REL_VEC = 768
VMEM_LIMIT = 56 * 1024 * 1024

F32 = jnp.float32
BF16 = jnp.bfloat16


def _rms_bf16(x, g):
    ms = jnp.mean(x * x, axis=-1, keepdims=True)
    return ((x * lax.rsqrt(ms + RMS_EPS)) * g).astype(BF16)


def _dot(a, b):
    return jnp.dot(a, b, preferred_element_type=F32)


def _dot_nt(a, b):
    return lax.dot_general(a, b, (((1,), (1,)), ((), ())), preferred_element_type=F32)


def _const_spec(shape):
    nd = len(shape)
    return pl.BlockSpec(shape, lambda *_: (0,) * nd, pipeline_mode=pl.Buffered(1))


def _params(sem):
    return pltpu.CompilerParams(dimension_semantics=sem, vmem_limit_bytes=VMEM_LIMIT)


def _ffn_kernel(x_ref, g_ref, wg_ref, wu_ref, wo_ref, gf_ref, o_ref, *, final_norm):
    x = x_ref[...]
    h = _rms_bf16(x, g_ref[...])
    gate = _dot(h, wg_ref[...])
    up = _dot(h, wu_ref[...])
    a = (gate * jax.nn.sigmoid(gate) * up).astype(BF16)
    y = x + 0.5 * _dot(a, wo_ref[...])
    if final_norm:
        ms = jnp.mean(y * y, axis=-1, keepdims=True)
        y = (y * lax.rsqrt(ms + RMS_EPS)) * gf_ref[...]
    o_ref[...] = y


def _ffn(x, g, wg, wu, wo, gf, *, final_norm, tm=512):
    n, d = x.shape
    row = pl.BlockSpec((tm, d), lambda i: (i, 0))
    return pl.pallas_call(
        functools.partial(_ffn_kernel, final_norm=final_norm),
        grid=(n // tm,),
        in_specs=[row, _const_spec(g.shape), _const_spec(wg.shape), _const_spec(wu.shape),
                  _const_spec(wo.shape), _const_spec(gf.shape)],
        out_specs=row,
        out_shape=jax.ShapeDtypeStruct((n, d), F32),
        compiler_params=_params(("parallel",)),
        name="ffn",
    )(x, g, wg, wu, wo, gf)


def _inproj_kernel(x_ref, g_ref, w_ref, b_ref, wf_ref, bf_ref, qkv_ref, fc_ref, carry_ref, *, tm):
    t = pl.program_id(1)
    h = _rms_bf16(x_ref[...], g_ref[...])
    qkv_ref[...] = (_dot(h, w_ref[...]) + b_ref[...]).astype(BF16)
    f = _dot(h, wf_ref[...]) + bf_ref[...]
    lf = (jnp.minimum(f, 0.0) - jnp.log1p(jnp.exp(-jnp.abs(f)))) * LOG2E
    s = lf.T[:8, :]
    lane = lax.broadcasted_iota(jnp.int32, s.shape, 1)
    sh = 1
    while sh < tm:
        s = s + jnp.where(lane >= sh, pltpu.roll(s, sh, axis=1), 0.0)
        sh *= 2

    @pl.when(t == 0)
    def _():
        carry_ref[...] = jnp.zeros_like(carry_ref)

    fc = s + carry_ref[:, 0:1]
    fc_ref[0] = fc
    carry_ref[...] = jnp.broadcast_to(fc[:, tm - 1:tm], carry_ref.shape)


def _inproj(x, g, w, b, wf, bfg, *, batch, tm=512):
    n, d = x.shape
    seq = n // batch
    nt = seq // tm
    return pl.pallas_call(
        functools.partial(_inproj_kernel, tm=tm),
        grid=(batch, nt),
        in_specs=[pl.BlockSpec((tm, d), lambda bi, ti: (bi * nt + ti, 0)),
                  _const_spec(g.shape), _const_spec(w.shape), _const_spec(b.shape),
                  _const_spec(wf.shape), _const_spec(bfg.shape)],
        out_specs=[pl.BlockSpec((tm, QKV_WIDTH), lambda bi, ti: (bi * nt + ti, 0)),
                   pl.BlockSpec((1, 8, tm), lambda bi, ti: (bi, 0, ti))],
        out_shape=[jax.ShapeDtypeStruct((n, QKV_WIDTH), BF16),
                   jax.ShapeDtypeStruct((batch, 8, seq), F32)],
        scratch_shapes=[pltpu.VMEM((8, LANES), F32)],
        compiler_params=_params(("parallel", "arbitrary")),
        name="inproj",
    )(x, g, w, b, wf, bfg)


def _head_masks(shape):
    lane = lax.broadcasted_iota(jnp.int32, shape, 1)
    lo = lane < HEAD_DIM
    return lo, jnp.logical_not(lo)


def _tile_schedule(n_tiles, diag_first):
    rows = []
    for i in range(n_tiles):
        kbs = range(i, -1, -1) if diag_first else range(i + 1)
        rows += [(i, kb, int(kb == i)) for kb in kbs]
    return np.asarray(rows, np.int32).T.copy()


def _run_pipeline(n_steps, stage0, stage1, stage2):
    stage0(0, 0)
    stage0(1, 1)
    stage1(0, 0)

    def body(h, c):
        t = 2 * h
        stage0(t, 0)
        stage1(t - 1, 1)
        stage2(t - 2, 0)
        stage0(t + 1, 1)
        stage1(t, 0)
        stage2(t - 1, 1)
        return c

    lax.fori_loop(1, n_steps // 2, body, 0)
    stage1(n_steps - 1, 1)
    stage2(n_steps - 2, 0)
    stage2(n_steps - 1, 1)


def _sb_kernel(tbl_ref, q_ref, k_ref, v_ref, o_ref, mask_ref, tri_ref,
               zs00, zs01, zs10, zs11, sp00, sp01, sp10, sp11, c00, c01, c10, c11, w00, w01, w10, w11,
               carry0, carry1, acc0, acc1, *, tq, n_blocks):
    zs_refs, sp_refs = ((zs00, zs01), (zs10, zs11)), ((sp00, sp01), (sp10, sp11))
    c_refs, w_refs = ((c00, c01), (c10, c11)), ((w00, w01), (w10, w11))
    carry_refs, acc_refs = (carry0, carry1), (acc0, acc1)
    row = lax.broadcasted_iota(jnp.int32, (tq, tq), 0)
    col = lax.broadcasted_iota(jnp.int32, (tq, tq), 1)
    mask_ref[0] = jnp.zeros((tq, tq), F32)
    mask_ref[1] = jnp.where(col < row, 0.0, NEG)
    tri_ref[...] = jnp.where(row > col, 1.0, 0.0).astype(BF16)
    masks = _head_masks((tq, LANES))
    for e in range(2):
        carry_refs[e][...] = jnp.zeros(carry_refs[e].shape, F32)
        acc_refs[e][...] = jnp.zeros(acc_refs[e].shape, F32)

    def rows_of(n):
        return pl.ds(pl.multiple_of(tbl_ref[0, n] * tq, tq), tq)

    def keys_of(n):
        return pl.ds(pl.multiple_of(tbl_ref[1, n] * tq, tq), tq)

    def logits(n, slot):
        rq = rows_of(n)
        q = q_ref[0, rq, :]
        k = k_ref[0, keys_of(n), :]
        mask = mask_ref[tbl_ref[2, n]]
        for e in range(2):
            qm = jnp.where(masks[e], q, jnp.zeros_like(q))
            z = _dot_nt(qm, k) + mask
            neg_abs = pltpu.bitcast(pltpu.bitcast(z, jnp.uint32) | jnp.uint32(0x80000000), F32)
            sp = jnp.maximum(z, 0.0) + jnp.log2(1.0 + jnp.exp2(neg_abs))
            zs_refs[slot][e][...] = z - sp
            sp_refs[slot][e][...] = sp.astype(BF16)
            carry = carry_refs[e][rq, :]
            c_refs[slot][e][...] = carry
            carry_refs[e][rq, :] = carry + jnp.sum(sp, axis=1, keepdims=True)

    def weights(n, slot):
        for e in range(2):
            right = _dot(sp_refs[slot][e][...], tri_ref[...])
            right = right + jnp.concatenate([c_refs[slot][e][...]] * (tq // LANES), axis=1)
            w_refs[slot][e][...] = jnp.exp2(zs_refs[slot][e][...] - right).astype(BF16)

    def pv(n, slot):
        rq = rows_of(n)
        v = v_ref[0, keys_of(n), :]
        for e in range(2):
            acc_refs[e][rq, :] = acc_refs[e][rq, :] + _dot(w_refs[slot][e][...], v)

    _run_pipeline(n_blocks, logits, weights, pv)
    full = _head_masks(acc_refs[0].shape)
    o_ref[0] = jnp.where(full[0], acc_refs[0][...], acc_refs[1][...]).astype(BF16)


def _sb_attention(qkv, *, tq=256):
    batch, seq, _ = qkv.shape
    npair = W_SB // LANES
    tbl = _tile_schedule(seq // tq, diag_first=True)
    n_blocks = tbl.shape[1]
    assert n_blocks % 2 == 0
    blk = lambda off: pl.BlockSpec((1, seq, LANES), lambda b, p, t: (b, 0, off + p))
    zs_shape, sp_shape = pltpu.VMEM((tq, tq), F32), pltpu.VMEM((tq, tq), BF16)
    c_shape, w_shape = pltpu.VMEM((tq, LANES), F32), pltpu.VMEM((tq, tq), BF16)
    acc_shape = pltpu.VMEM((seq, LANES), F32)
    grid_spec = pltpu.PrefetchScalarGridSpec(
        num_scalar_prefetch=1,
        grid=(batch, npair),
        in_specs=[blk(0), blk(npair), blk(2 * npair)],
        out_specs=pl.BlockSpec((1, seq, LANES), lambda b, p, t: (b, 0, p)),
        scratch_shapes=[pltpu.VMEM((2, tq, tq), F32), pltpu.VMEM((tq, tq), BF16)]
                       + [zs_shape] * 4 + [sp_shape] * 4 + [c_shape] * 4 + [w_shape] * 4 + [acc_shape] * 4,
    )
    return pl.pallas_call(
        functools.partial(_sb_kernel, tq=tq, n_blocks=n_blocks),
        grid_spec=grid_spec,
        out_shape=jax.ShapeDtypeStruct((batch, seq, W_SB), BF16),
        compiler_params=_params(("parallel", "parallel")),
        name="sb_attn",
    )(jnp.asarray(tbl), qkv, qkv, qkv)


def _fox_kernel(tbl_ref, q_ref, k_ref, v_ref, f_ref, o_ref, mask_ref,
                s00, s01, s10, s11, p00, p01, p10, p11, a00, a01, a10, a11,
                m0, m1, acc0, acc1, va0, va1, *, tq, n_blocks):
    s_refs, p_refs, a_refs = ((s00, s01), (s10, s11)), ((p00, p01), (p10, p11)), ((a00, a01), (a10, a11))
    m_refs, acc_refs, va_refs = (m0, m1), (acc0, acc1), (va0, va1)
    row = lax.broadcasted_iota(jnp.int32, (tq, tq), 0)
    col = lax.broadcasted_iota(jnp.int32, (tq, tq), 1)
    mask_ref[0] = jnp.zeros((tq, tq), F32)
    mask_ref[1] = jnp.where(col <= row, 0.0, NEG)
    masks = _head_masks((tq, LANES))
    v_all = v_ref[0]
    full = _head_masks(v_all.shape)
    for e in range(2):
        m_refs[e][...] = jnp.full(m_refs[e].shape, NEG, F32)
        acc_refs[e][...] = jnp.zeros(acc_refs[e].shape, F32)
        va_refs[e][...] = jnp.where(full[e], v_all, jnp.ones_like(v_all))

    def rows_of(n):
        return pl.ds(pl.multiple_of(tbl_ref[0, n] * tq, tq), tq)

    def keys_of(n):
        return pl.ds(pl.multiple_of(tbl_ref[1, n] * tq, tq), tq)

    def qk(n, slot):
        q = q_ref[0, rows_of(n), :]
        k = k_ref[0, keys_of(n), :]
        for e in range(2):
            qm = jnp.where(masks[e], q, jnp.zeros_like(q))
            s_refs[slot][e][...] = _dot_nt(qm, k)

    def sm(n, slot):
        rq, rk = rows_of(n), keys_of(n)
        mask = mask_ref[tbl_ref[2, n]]
        for e in range(2):
            fq0 = f_ref[e, :, rq][:, 0:1]
            s = s_refs[slot][e][...] + (fq0 - f_ref[e, :, rk]) + mask
            m_prev = m_refs[e][rq, :]
            m_new = jnp.maximum(m_prev, jnp.max(s, axis=1, keepdims=True))
            m_refs[e][rq, :] = m_new
            a_refs[slot][e][...] = jnp.exp2(m_prev - m_new)
            p_refs[slot][e][...] = jnp.exp2(s - jnp.concatenate([m_new] * (tq // LANES), axis=1)).astype(BF16)

    def pv(n, slot):
        rq, rk = rows_of(n), keys_of(n)
        for e in range(2):
            acc_refs[e][rq, :] = (a_refs[slot][e][...] * acc_refs[e][rq, :]
                                  + _dot(p_refs[slot][e][...], va_refs[e][rk, :]))

    _run_pipeline(n_blocks, qk, sm, pv)
    outs = []
    for e in range(2):
        acc = acc_refs[e][...]
        outs.append(acc * (1.0 / pltpu.roll(acc, HEAD_DIM, axis=1)))
    o_ref[0] = jnp.where(full[0], outs[0], outs[1]).astype(BF16)


def _fox_attention(qkv, fcum, *, tq=256):
    batch, seq, _ = qkv.shape
    cb = (3 * W_SB + 3 * W_CH) // LANES
    npair = W_FOX // LANES
    tbl = _tile_schedule(seq // tq, diag_first=False)
    n_blocks = tbl.shape[1]
    assert n_blocks % 2 == 0
    blk = lambda off: pl.BlockSpec((1, seq, LANES), lambda b, p, t: (b, 0, cb + off + p))
    s_shape, p_shape, a_shape = pltpu.VMEM((tq, tq), F32), pltpu.VMEM((tq, tq), BF16), pltpu.VMEM((tq, LANES), F32)
    acc_shape, v_shape = pltpu.VMEM((seq, LANES), F32), pltpu.VMEM((seq, LANES), BF16)
    grid_spec = pltpu.PrefetchScalarGridSpec(
        num_scalar_prefetch=1,
        grid=(batch, npair),
        in_specs=[blk(0), blk(npair), blk(2 * npair),
                  pl.BlockSpec((None, 2, 1, seq), lambda b, p, t: (b, p, 0, 0))],
        out_specs=pl.BlockSpec((1, seq, LANES), lambda b, p, t: (b, 0, p)),
        scratch_shapes=[pltpu.VMEM((2, tq, tq), F32)] + [s_shape] * 4 + [p_shape] * 4 + [a_shape] * 4
                       + [acc_shape] * 4 + [v_shape] * 2,
    )
    return pl.pallas_call(
        functools.partial(_fox_kernel, tq=tq, n_blocks=n_blocks),
        grid_spec=grid_spec,
        out_shape=jax.ShapeDtypeStruct((batch, seq, W_FOX), BF16),
        compiler_params=_params(("parallel", "parallel")),
        name="fox_attn",
    )(jnp.asarray(tbl), qkv, qkv, qkv, fcum)


def _chunk_kernel(q_ref, k_ref, v_ref, r_ref, o_ref, bias_ref, kpad_ref, vpad_ref,
                  s0_ref, s1_ref, p0_ref, p1_ref, l0_ref, l1_ref, *, seq):
    n_tiles = seq // PAIR_Q
    pad = LEFT_CHUNKS * CHUNK
    rows = lax.broadcasted_iota(jnp.int32, (PAIR_Q, PAIR_KEYS), 0)
    cols = lax.broadcasted_iota(jnp.int32, (PAIR_Q, PAIR_KEYS), 1)
    first_key = jnp.where(rows < CHUNK, 0, CHUNK)
    valid = (cols >= first_key) & (cols < first_key + PAIR_KEYS - CHUNK)
    masks = _head_masks((PAIR_Q, LANES))
    for e in range(2):
        gen = jnp.broadcast_to(r_ref[0, e:e + 1, :], (PAIR_Q, REL_VEC))
        toeplitz = pltpu.roll(gen, 0, 1, stride=1, stride_axis=0)
        bias_ref[e] = jnp.where(valid, toeplitz[:, :PAIR_KEYS], NEG)
    kpad_ref[:pad, :] = jnp.zeros((pad, LANES), BF16)
    vpad_ref[:pad, :] = jnp.zeros((pad, LANES), BF16)
    kpad_ref[pad:, :] = k_ref[0]
    vpad_ref[pad:, :] = v_ref[0]
    s_refs, p_refs, l_refs = (s0_ref, s1_ref), (p0_ref, p1_ref), (l0_ref, l1_ref)

    def qk(t, slot):
        r0 = pl.multiple_of(t * PAIR_Q, PAIR_Q)
        q = q_ref[0, pl.ds(r0, PAIR_Q), :]
        k = kpad_ref[pl.ds(r0, PAIR_KEYS), :]
        for e in range(2):
            qm = jnp.where(masks[e], q, jnp.zeros_like(q))
            s_refs[slot][e] = _dot_nt(qm, k)

    def sm(t, slot):
        lim = pad - t * PAIR_Q
        for e in range(2):
            s = jnp.where(cols >= lim, s_refs[slot][e] + bias_ref[e], NEG)
            p = jnp.exp2(s - jnp.max(s, axis=1, keepdims=True))
            l_refs[slot][e] = jnp.sum(p, axis=1, keepdims=True)
            p_refs[slot][e] = p.astype(BF16)

    def pv(t, slot):
        r0 = pl.multiple_of(t * PAIR_Q, PAIR_Q)
        v = vpad_ref[pl.ds(r0, PAIR_KEYS), :]
        outs = [_dot(p_refs[slot][e], v) * (1.0 / l_refs[slot][e]) for e in range(2)]
        o_ref[0, pl.ds(r0, PAIR_Q), :] = jnp.where(masks[0], outs[0], outs[1]).astype(BF16)

    _run_pipeline(n_tiles, qk, sm, pv)


def _chunk_attention(qkv, rel_gen):
    batch, seq, _ = qkv.shape
    cb = 3 * W_SB // LANES
    npair = W_CH // LANES
    assert (seq // PAIR_Q) % 2 == 0
    blk = lambda off: pl.BlockSpec((1, seq, LANES), lambda b, p: (b, 0, cb + off + p))
    pad = LEFT_CHUNKS * CHUNK
    s_shape = pltpu.VMEM((2, PAIR_Q, PAIR_KEYS), F32)
    p_shape = pltpu.VMEM((2, PAIR_Q, PAIR_KEYS), BF16)
    l_shape = pltpu.VMEM((2, PAIR_Q, 1), F32)
    return pl.pallas_call(
        functools.partial(_chunk_kernel, seq=seq),
        grid=(batch, npair),
        in_specs=[blk(0), blk(npair), blk(2 * npair),
                  pl.BlockSpec((1, 2, REL_VEC), lambda b, p: (p, 0, 0))],
        out_specs=pl.BlockSpec((1, seq, LANES), lambda b, p: (b, 0, p)),
        out_shape=jax.ShapeDtypeStruct((batch, seq, W_CH), BF16),
        scratch_shapes=[pltpu.VMEM((2, PAIR_Q, PAIR_KEYS), F32),
                        pltpu.VMEM((seq + pad, LANES), BF16), pltpu.VMEM((seq + pad, LANES), BF16),
                        s_shape, s_shape, p_shape, p_shape, l_shape, l_shape],
        compiler_params=_params(("parallel", "parallel")),
        name="chunk_attn",
    )(qkv, qkv, qkv, rel_gen)


def _merge_kernel(x_ref, oa_ref, ob_ref, oc_ref, g_ref, wg_ref, bg_ref, wa_ref, wb_ref, wc_ref, wo_ref, o_ref):
    x = x_ref[...]
    h = _rms_bf16(x, g_ref[...])
    gates = jax.nn.sigmoid(_dot(h, wg_ref[...]) + bg_ref[...])
    merged = (gates[:, :D_MODEL] * _dot(oa_ref[...], wa_ref[...])
              + gates[:, D_MODEL:2 * D_MODEL] * _dot(ob_ref[...], wb_ref[...])
              + gates[:, 2 * D_MODEL:] * _dot(oc_ref[...], wc_ref[...]))
    o_ref[...] = x + _dot(merged.astype(BF16), wo_ref[...])


def _merge(x, oa, ob, oc, g, wg, bg, wa, wb, wc, wo, *, tm=512):
    n, d = x.shape
    rows = lambda w: pl.BlockSpec((tm, w), lambda i: (i, 0))
    consts = [g, wg, bg, wa, wb, wc, wo]
    return pl.pallas_call(
        _merge_kernel,
        grid=(n // tm,),
        in_specs=[rows(d), rows(oa.shape[1]), rows(ob.shape[1]), rows(oc.shape[1])]
                 + [_const_spec(c.shape) for c in consts],
        out_specs=rows(d),
        out_shape=jax.ShapeDtypeStruct((n, d), F32),
        compiler_params=_params(("parallel",)),
        name="merge",
    )(x, oa, ob, oc, *consts)


def _rel_generator_index():
    j = np.arange(REL_VEC)
    dist = np.where(j <= PAIR_KEYS, LEFT_CHUNKS * CHUNK - j, MAX_REL)
    return np.clip(dist, -MAX_REL, MAX_REL) + MAX_REL


def _prep_layer(w_in, b_in, rel_bias):
    qscale = HEAD_DIM ** -0.5 * LOG2E
    col_scale = np.ones((QKV_WIDTH,), np.float32)
    for start, width in ((0, W_SB), (3 * W_SB, W_CH), (3 * W_SB + 3 * W_CH, W_FOX)):
        col_scale[start:start + width] = qscale
    col_scale = jnp.asarray(col_scale)
    w_qkv = (w_in[:, :QKV_WIDTH] * col_scale).astype(BF16)
    b_qkv = (b_in[:QKV_WIDTH] * col_scale)[None, :]
    nf = N_HEADS_FOX
    w_f = jnp.pad(w_in[:, QKV_WIDTH:QKV_WIDTH + nf], ((0, 0), (0, LANES - nf))).astype(BF16)
    b_f = jnp.pad(b_in[QKV_WIDTH:QKV_WIDTH + nf], (0, LANES - nf))[None, :]
    w_g = w_in[:, QKV_WIDTH + nf:].astype(BF16)
    b_g = b_in[QKV_WIDTH + nf:][None, :]
    rel_gen = (rel_bias[_rel_generator_index(), :] * LOG2E).T.reshape(N_HEADS_CH // 2, 2, REL_VEC)
    return w_qkv, b_qkv, w_f, b_f, w_g, b_g, rel_gen


def kernel(x, g_ffn1, w_ffn1_in, w_ffn1_out, g_mix, w_in, b_in, rel_bias, w_br_sb, w_br_ch, w_br_fox, w_out,
           g_ffn2, w_ffn2_in, w_ffn2_out, g_final):
    batch, seq, d = x.shape
    depth = g_ffn1.shape[0]
    d_ff = w_ffn1_out.shape[1]
    xf = x.reshape(batch * seq, d)
    gf = g_final[None, :]
    for l in range(depth):
        xf = _ffn(xf, g_ffn1[l][None, :], w_ffn1_in[l, :, :d_ff].astype(BF16), w_ffn1_in[l, :, d_ff:].astype(BF16),
                  w_ffn1_out[l].astype(BF16), gf, final_norm=False)
        w_qkv, b_qkv, w_f, b_f, w_g, b_g, rel_gen = _prep_layer(w_in[l], b_in[l], rel_bias[l])
        qkv, fcum = _inproj(xf, g_mix[l][None, :], w_qkv, b_qkv, w_f, b_f, batch=batch)
        qkv = qkv.reshape(batch, seq, QKV_WIDTH)
        o_a = _sb_attention(qkv)
        o_b = _chunk_attention(qkv, rel_gen)
        o_c = _fox_attention(qkv, fcum[:, :N_HEADS_FOX].reshape(batch, N_HEADS_FOX, 1, seq))
        xf = _merge(xf, o_a.reshape(batch * seq, W_SB), o_b.reshape(batch * seq, W_CH),
                    o_c.reshape(batch * seq, W_FOX), g_mix[l][None, :], w_g, b_g,
                    w_br_sb[l].astype(BF16), w_br_ch[l].astype(BF16), w_br_fox[l].astype(BF16),
                    w_out[l].astype(BF16))
        xf = _ffn(xf, g_ffn2[l][None, :], w_ffn2_in[l, :, :d_ff].astype(BF16), w_ffn2_in[l, :, d_ff:].astype(BF16),
                  w_ffn2_out[l].astype(BF16), gf, final_norm=(l == depth - 1))
    return xf.reshape(batch, seq, d)
```

```python
import functools

import numpy as np
import jax
import jax.numpy as jnp
from jax import lax
from jax.experimental import pallas as pl
from jax.experimental.pallas import tpu as pltpu

D_MODEL = 1024
HEAD_DIM = 64
CHUNK = 64
LEFT_CHUNKS = 8
MAX_REL = 128
N_HEADS_SB = 4
N_HEADS_CH = 8
N_HEADS_FOX = 4
W_SB = N_HEADS_SB * HEAD_DIM
W_CH = N_HEADS_CH * HEAD_DIM
W_FOX = N_HEADS_FOX * HEAD_DIM
QKV_WIDTH = 3 * (W_SB + W_CH + W_FOX)
RMS_EPS = 1e-6
NEG = -1e30
LOG2E = 1.4426950408889634

LANES = 128
PAIR_Q = 2 * CHUNK
PAIR_KEYS = PAIR_Q + LEFT_CHUNKS * CHUNK
REL_VEC = 768
VMEM_LIMIT = 56 * 1024 * 1024

F32 = jnp.float32
BF16 = jnp.bfloat16


def _rms_bf16(x, g):
    ms = jnp.mean(x * x, axis=-1, keepdims=True)
    return ((x * lax.rsqrt(ms + RMS_EPS)) * g).astype(BF16)


def _dot(a, b):
    return jnp.dot(a, b, preferred_element_type=F32)


def _dot_nt(a, b):
    return lax.dot_general(a, b, (((1,), (1,)), ((), ())), preferred_element_type=F32)


def _const_spec(shape):
    nd = len(shape)
    return pl.BlockSpec(shape, lambda *_: (0,) * nd, pipeline_mode=pl.Buffered(1))


def _layer_spec(w, layer, col_blocks=1, col=0):
    _, r, c = w.shape
    return pl.BlockSpec((None, r, c // col_blocks), lambda *_: (layer, 0, col), pipeline_mode=pl.Buffered(1))


def _params(sem):
    return pltpu.CompilerParams(dimension_semantics=sem, vmem_limit_bytes=VMEM_LIMIT)


def _ffn_kernel(x_ref, g_ref, wg_ref, wu_ref, wo_ref, gf_ref, o_ref, *, final_norm):
    x = x_ref[...]
    h = _rms_bf16(x, g_ref[...])
    gate = _dot(h, wg_ref[...])
    up = _dot(h, wu_ref[...])
    a = (gate * jax.nn.sigmoid(gate) * up).astype(BF16)
    y = x + 0.5 * _dot(a, wo_ref[...])
    if final_norm:
        ms = jnp.mean(y * y, axis=-1, keepdims=True)
        y = (y * lax.rsqrt(ms + RMS_EPS)) * gf_ref[...]
    o_ref[...] = y


def _ffn(x, g, w_in, w_out, gf, *, layer, final_norm, tm=512):
    n, d = x.shape
    row = pl.BlockSpec((tm, d), lambda i: (i, 0))
    return pl.pallas_call(
        functools.partial(_ffn_kernel, final_norm=final_norm),
        grid=(n // tm,),
        in_specs=[row, _const_spec(g.shape), _layer_spec(w_in, layer, 2, 0), _layer_spec(w_in, layer, 2, 1),
                  _layer_spec(w_out, layer), _const_spec(gf.shape)],
        out_specs=row,
        out_shape=jax.ShapeDtypeStruct((n, d), F32),
        compiler_params=_params(("parallel",)),
        name="ffn",
    )(x, g, w_in, w_in, w_out, gf)


def _inproj_kernel(x_ref, g_ref, w_ref, b_ref, wf_ref, bf_ref, qkv_ref, fc_ref, carry_ref, *, tm):
    @pl.when(pl.program_id(1) == 0)
    def _():
        carry_ref[...] = jnp.zeros_like(carry_ref)

    h = _rms_bf16(x_ref[...], g_ref[...])
    f = _dot(h, wf_ref[...]) + bf_ref[...]
    lf = (jnp.minimum(f, 0.0) - jnp.log1p(jnp.exp(-jnp.abs(f)))) * LOG2E
    s = lf.T[:8, :]
    lane = lax.broadcasted_iota(jnp.int32, s.shape, 1)
    sh = 1
    while sh < tm:
        s = s + jnp.where(lane >= sh, pltpu.roll(s, sh, axis=1), 0.0)
        sh *= 2
    fc = s + carry_ref[:, 0:1]
    fc_ref[0] = fc
    carry_ref[...] = jnp.broadcast_to(fc[:, tm - 1:tm], carry_ref.shape)
    qkv_ref[...] = (_dot(h, w_ref[...]) + b_ref[...]).astype(BF16)


def _inproj(x, g, w, b, wf, bfg, *, layer, batch, tm=512):
    n, d = x.shape
    seq = n // batch
    nt = seq // tm
    return pl.pallas_call(
        functools.partial(_inproj_kernel, tm=tm),
        grid=(batch, nt),
        in_specs=[pl.BlockSpec((tm, d), lambda bi, ti: (bi * nt + ti, 0)),
                  _const_spec(g.shape), _layer_spec(w, layer), _const_spec(b.shape),
                  _layer_spec(wf, layer), _const_spec(bfg.shape)],
        out_specs=[pl.BlockSpec((tm, QKV_WIDTH), lambda bi, ti: (bi * nt + ti, 0)),
                   pl.BlockSpec((1, 8, tm), lambda bi, ti: (bi, 0, ti))],
        out_shape=[jax.ShapeDtypeStruct((n, QKV_WIDTH), BF16),
                   jax.ShapeDtypeStruct((batch, 8, seq), F32)],
        scratch_shapes=[pltpu.VMEM((8, LANES), F32)],
        compiler_params=_params(("parallel", "arbitrary")),
        name="inproj",
    )(x, g, w, b, wf, bfg)


def _head_masks(shape):
    lane = lax.broadcasted_iota(jnp.int32, shape, 1)
    lo = lane < HEAD_DIM
    return lo, jnp.logical_not(lo)


def _tile_schedule(n_tiles, diag_first):
    rows = []
    for i in range(n_tiles):
        kbs = range(i, -1, -1) if diag_first else range(i + 1)
        rows += [(i, kb, int(kb == i)) for kb in kbs]
    return np.asarray(rows, np.int32).T.copy()


def _tile_index(tbl_ref, table, tq):
    def entry(row, n):
        return int(table[row, n]) if isinstance(n, int) else tbl_ref[row, n]

    def tile(row, n):
        i = entry(row, n)
        return pl.ds(i * tq, tq) if isinstance(i, int) else pl.ds(pl.multiple_of(i * tq, tq), tq)

    return entry, tile


def _run_pipeline(n_steps, stage0, stage1, stage2, trip_ticks):
    def tick(t, parity):
        stage0(t, parity)
        stage1(t - 1, 1 - parity)
        stage2(t - 2, parity)

    stage0(0, 0)
    stage0(1, 1)
    stage1(0, 0)
    trips = (n_steps - 2) // trip_ticks
    if trips > 1:
        def body(h, c):
            for j in range(trip_ticks):
                tick(2 + h * trip_ticks + j, j % 2)
            return c
        lax.fori_loop(0, trips, body, 0)
        done = 2 + trips * trip_ticks
    else:
        done = 2
    for t in range(done, n_steps):
        tick(t, t % 2)
    stage1(n_steps - 1, (n_steps - 1) % 2)
    stage2(n_steps - 2, n_steps % 2)
    stage2(n_steps - 1, (n_steps - 1) % 2)


def _sb_kernel(tbl_ref, q_ref, k_ref, v_ref, o_ref, mask_ref, tri_ref,
               zs00, zs01, zs10, zs11, sp00, sp01, sp10, sp11, c00, c01, c10, c11, w00, w01, w10, w11,
               carry0, carry1, acc0, acc1, *, tq, n_blocks, table, trip_ticks):
    zs_refs, sp_refs = ((zs00, zs01), (zs10, zs11)), ((sp00, sp01), (sp10, sp11))
    c_refs, w_refs = ((c00, c01), (c10, c11)), ((w00, w01), (w10, w11))
    carry_refs, acc_refs = (carry0, carry1), (acc0, acc1)
    row = lax.broadcasted_iota(jnp.int32, (tq, tq), 0)
    col = lax.broadcasted_iota(jnp.int32, (tq, tq), 1)
    mask_ref[0] = jnp.zeros((tq, tq), F32)
    mask_ref[1] = jnp.where(col < row, 0.0, NEG)
    tri_ref[...] = jnp.where(row > col, 1.0, 0.0).astype(BF16)
    masks = _head_masks((tq, LANES))
    for e in range(2):
        carry_refs[e][...] = jnp.zeros(carry_refs[e].shape, F32)
        acc_refs[e][...] = jnp.zeros(acc_refs[e].shape, F32)

    entry, tile = _tile_index(tbl_ref, table, tq)
    rows_of, keys_of = functools.partial(tile, 0), functools.partial(tile, 1)

    def logits(n, slot):
        rq = rows_of(n)
        q = q_ref[0, rq, :]
        k = k_ref[0, keys_of(n), :]
        mask = mask_ref[entry(2, n)]
        for e in range(2):
            qm = jnp.where(masks[e], q, jnp.zeros_like(q))
            z = _dot_nt(qm, k) + mask
            sp = jnp.maximum(z, 0.0) + jnp.log2(1.0 + jnp.exp2(-jnp.abs(z)))
            zs_refs[slot][e][...] = z - sp
            sp_refs[slot][e][...] = sp.astype(BF16)
            carry = carry_refs[e][rq, :]
            c_refs[slot][e][...] = carry
            carry_refs[e][rq, :] = carry + jnp.sum(sp, axis=1, keepdims=True)

    def weights(n, slot):
        for e in range(2):
            right = _dot(sp_refs[slot][e][...], tri_ref[...])
            right = right + jnp.concatenate([c_refs[slot][e][...]] * (tq // LANES), axis=1)
            w_refs[slot][e][...] = jnp.exp2(zs_refs[slot][e][...] - right).astype(BF16)

    def pv(n, slot):
        rq = rows_of(n)
        v = v_ref[0, keys_of(n), :]
        for e in range(2):
            acc_refs[e][rq, :] = acc_refs[e][rq, :] + _dot(w_refs[slot][e][...], v)

    _run_pipeline(n_blocks, logits, weights, pv, trip_ticks)
    full = _head_masks(acc_refs[0].shape)
    o_ref[0] = jnp.where(full[0], acc_refs[0][...], acc_refs[1][...]).astype(BF16)


def _sb_attention(qkv, *, tq=256):
    batch, seq, _ = qkv.shape
    npair = W_SB // LANES
    tbl = _tile_schedule(seq // tq, diag_first=True)
    n_blocks = tbl.shape[1]
    assert n_blocks % 2 == 0
    blk = lambda off: pl.BlockSpec((1, seq, LANES), lambda b, p, t: (b, 0, off + p))
    zs_shape, sp_shape = pltpu.VMEM((tq, tq), F32), pltpu.VMEM((tq, tq), BF16)
    c_shape, w_shape = pltpu.VMEM((tq, LANES), F32), pltpu.VMEM((tq, tq), BF16)
    acc_shape = pltpu.VMEM((seq, LANES), F32)
    grid_spec = pltpu.PrefetchScalarGridSpec(
        num_scalar_prefetch=1,
        grid=(batch, npair),
        in_specs=[blk(0), blk(npair), blk(2 * npair)],
        out_specs=pl.BlockSpec((1, seq, LANES), lambda b, p, t: (b, 0, p)),
        scratch_shapes=[pltpu.VMEM((2, tq, tq), F32), pltpu.VMEM((tq, tq), BF16)]
                       + [zs_shape] * 4 + [sp_shape] * 4 + [c_shape] * 4 + [w_shape] * 4 + [acc_shape] * 4,
    )
    return pl.pallas_call(
        functools.partial(_sb_kernel, tq=tq, n_blocks=n_blocks, table=tbl, trip_ticks=n_blocks - 2),
        grid_spec=grid_spec,
        out_shape=jax.ShapeDtypeStruct((batch, seq, W_SB), BF16),
        compiler_params=_params(("parallel", "parallel")),
        name="sb_attn",
    )(jnp.asarray(tbl), qkv, qkv, qkv)


def _fox_kernel(tbl_ref, q_ref, k_ref, v_ref, f_ref, o_ref, mask_ref,
                s00, s01, s10, s11, p00, p01, p10, p11, a00, a01, a10, a11,
                m0, m1, acc0, acc1, va0, va1, *, tq, n_blocks, table, trip_ticks):
    s_refs, p_refs, a_refs = ((s00, s01), (s10, s11)), ((p00, p01), (p10, p11)), ((a00, a01), (a10, a11))
    m_refs, acc_refs, va_refs = (m0, m1), (acc0, acc1), (va0, va1)
    row = lax.broadcasted_iota(jnp.int32, (tq, tq), 0)
    col = lax.broadcasted_iota(jnp.int32, (tq, tq), 1)
    mask_ref[0] = jnp.zeros((tq, tq), F32)
    mask_ref[1] = jnp.where(col <= row, 0.0, NEG)
    masks = _head_masks((tq, LANES))
    v_all = v_ref[0]
    full = _head_masks(v_all.shape)
    for e in range(2):
        m_refs[e][...] = jnp.full(m_refs[e].shape, NEG, F32)
        acc_refs[e][...] = jnp.zeros(acc_refs[e].shape, F32)
        va_refs[e][...] = jnp.where(full[e], v_all, jnp.ones_like(v_all))

    entry, tile = _tile_index(tbl_ref, table, tq)
    rows_of, keys_of = functools.partial(tile, 0), functools.partial(tile, 1)

    def qk(n, slot):
        q = q_ref[0, rows_of(n), :]
        k = k_ref[0, keys_of(n), :]
        for e in range(2):
            qm = jnp.where(masks[e], q, jnp.zeros_like(q))
            s_refs[slot][e][...] = _dot_nt(qm, k)

    def sm(n, slot):
        rq, rk = rows_of(n), keys_of(n)
        mask = mask_ref[entry(2, n)]
        for e in range(2):
            fq0 = f_ref[e, :, rq][:, 0:1]
            s = s_refs[slot][e][...] + (fq0 - f_ref[e, :, rk]) + mask
            m_prev = m_refs[e][rq, :]
            m_new = jnp.maximum(m_prev, jnp.max(s, axis=1, keepdims=True))
            m_refs[e][rq, :] = m_new
            a_refs[slot][e][...] = jnp.exp2(m_prev - m_new)
            p_refs[slot][e][...] = jnp.exp2(s - jnp.concatenate([m_new] * (tq // LANES), axis=1)).astype(BF16)

    def pv(n, slot):
        rq, rk = rows_of(n), keys_of(n)
        for e in range(2):
            acc_refs[e][rq, :] = (a_refs[slot][e][...] * acc_refs[e][rq, :]
                                  + _dot(p_refs[slot][e][...], va_refs[e][rk, :]))

    _run_pipeline(n_blocks, qk, sm, pv, trip_ticks)
    outs = []
    for e in range(2):
        acc = acc_refs[e][...]
        outs.append(acc * (1.0 / pltpu.roll(acc, HEAD_DIM, axis=1)))
    o_ref[0] = jnp.where(full[0], outs[0], outs[1]).astype(BF16)


def _fox_attention(qkv, fcum, *, tq=256):
    batch, seq, _ = qkv.shape
    cb = (3 * W_SB + 3 * W_CH) // LANES
    npair = W_FOX // LANES
    tbl = _tile_schedule(seq // tq, diag_first=False)
    n_blocks = tbl.shape[1]
    assert n_blocks % 2 == 0
    blk = lambda off: pl.BlockSpec((1, seq, LANES), lambda b, p, t: (b, 0, cb + off + p))
    s_shape, p_shape, a_shape = pltpu.VMEM((tq, tq), F32), pltpu.VMEM((tq, tq), BF16), pltpu.VMEM((tq, LANES), F32)
    acc_shape, v_shape = pltpu.VMEM((seq, LANES), F32), pltpu.VMEM((seq, LANES), BF16)
    grid_spec = pltpu.PrefetchScalarGridSpec(
        num_scalar_prefetch=1,
        grid=(batch, npair),
        in_specs=[blk(0), blk(npair), blk(2 * npair),
                  pl.BlockSpec((None, 2, 1, seq), lambda b, p, t: (b, p, 0, 0))],
        out_specs=pl.BlockSpec((1, seq, LANES), lambda b, p, t: (b, 0, p)),
        scratch_shapes=[pltpu.VMEM((2, tq, tq), F32)] + [s_shape] * 4 + [p_shape] * 4 + [a_shape] * 4
                       + [acc_shape] * 4 + [v_shape] * 2,
    )
    return pl.pallas_call(
        functools.partial(_fox_kernel, tq=tq, n_blocks=n_blocks, table=tbl, trip_ticks=n_blocks - 2),
        grid_spec=grid_spec,
        out_shape=jax.ShapeDtypeStruct((batch, seq, W_FOX), BF16),
        compiler_params=_params(("parallel", "parallel")),
        name="fox_attn",
    )(jnp.asarray(tbl), qkv, qkv, qkv, fcum)


def _chunk_kernel(q_ref, k_ref, v_ref, r_ref, o_ref, bias_ref, kpad_ref, vpad_ref,
                  s0_ref, s1_ref, p0_ref, p1_ref, l0_ref, l1_ref, *, seq, trip_ticks):
    n_tiles = seq // PAIR_Q
    pad = LEFT_CHUNKS * CHUNK
    rows = lax.broadcasted_iota(jnp.int32, (PAIR_Q, PAIR_KEYS), 0)
    cols = lax.broadcasted_iota(jnp.int32, (PAIR_Q, PAIR_KEYS), 1)
    first_key = jnp.where(rows < CHUNK, 0, CHUNK)
    valid = (cols >= first_key) & (cols < first_key + PAIR_KEYS - CHUNK)
    masks = _head_masks((PAIR_Q, LANES))
    for e in range(2):
        gen = jnp.broadcast_to(r_ref[0, e:e + 1, :], (PAIR_Q, REL_VEC))
        toeplitz = pltpu.roll(gen, 0, 1, stride=1, stride_axis=0)
        bias_ref[e] = jnp.where(valid, toeplitz[:, :PAIR_KEYS], NEG)
    kpad_ref[:pad, :] = jnp.zeros((pad, LANES), BF16)
    vpad_ref[:pad, :] = jnp.zeros((pad, LANES), BF16)
    kpad_ref[pad:, :] = k_ref[0]
    vpad_ref[pad:, :] = v_ref[0]
    s_refs, p_refs, l_refs = (s0_ref, s1_ref), (p0_ref, p1_ref), (l0_ref, l1_ref)

    def qk(t, slot):
        r0 = t * PAIR_Q if isinstance(t, int) else pl.multiple_of(t * PAIR_Q, PAIR_Q)
        q = q_ref[0, pl.ds(r0, PAIR_Q), :]
        k = kpad_ref[pl.ds(r0, PAIR_KEYS), :]
        for e in range(2):
            qm = jnp.where(masks[e], q, jnp.zeros_like(q))
            s_refs[slot][e] = _dot_nt(qm, k)

    def sm(t, slot):
        lim = pad - t * PAIR_Q
        for e in range(2):
            s = jnp.where(cols >= lim, s_refs[slot][e] + bias_ref[e], NEG)
            p = jnp.exp2(s - jnp.max(s, axis=1, keepdims=True))
            l_refs[slot][e] = jnp.sum(p, axis=1, keepdims=True)
            p_refs[slot][e] = p.astype(BF16)

    def pv(t, slot):
        r0 = t * PAIR_Q if isinstance(t, int) else pl.multiple_of(t * PAIR_Q, PAIR_Q)
        v = vpad_ref[pl.ds(r0, PAIR_KEYS), :]
        outs = [_dot(p_refs[slot][e], v) * (1.0 / l_refs[slot][e]) for e in range(2)]
        o_ref[0, pl.ds(r0, PAIR_Q), :] = jnp.where(masks[0], outs[0], outs[1]).astype(BF16)

    _run_pipeline(n_tiles, qk, sm, pv, trip_ticks)


def _chunk_attention(qkv, rel_gen):
    batch, seq, _ = qkv.shape
    cb = 3 * W_SB // LANES
    npair = W_CH // LANES
    n_tiles = seq // PAIR_Q
    assert n_tiles % 2 == 0
    blk = lambda off: pl.BlockSpec((1, seq, LANES), lambda b, p: (b, 0, cb + off + p))
    pad = LEFT_CHUNKS * CHUNK
    s_shape = pltpu.VMEM((2, PAIR_Q, PAIR_KEYS), F32)
    p_shape = pltpu.VMEM((2, PAIR_Q, PAIR_KEYS), BF16)
    l_shape = pltpu.VMEM((2, PAIR_Q, 1), F32)
    return pl.pallas_call(
        functools.partial(_chunk_kernel, seq=seq, trip_ticks=n_tiles - 2),
        grid=(batch, npair),
        in_specs=[blk(0), blk(npair), blk(2 * npair),
                  pl.BlockSpec((1, 2, REL_VEC), lambda b, p: (p, 0, 0))],
        out_specs=pl.BlockSpec((1, seq, LANES), lambda b, p: (b, 0, p)),
        out_shape=jax.ShapeDtypeStruct((batch, seq, W_CH), BF16),
        scratch_shapes=[pltpu.VMEM((2, PAIR_Q, PAIR_KEYS), F32),
                        pltpu.VMEM((seq + pad, LANES), BF16), pltpu.VMEM((seq + pad, LANES), BF16),
                        s_shape, s_shape, p_shape, p_shape, l_shape, l_shape],
        compiler_params=_params(("parallel", "parallel")),
        name="chunk_attn",
    )(qkv, qkv, qkv, rel_gen)


def _merge_kernel(x_ref, oa_ref, ob_ref, oc_ref, g_ref, wg_ref, bg_ref, wa_ref, wb_ref, wc_ref, wo_ref, o_ref):
    x = x_ref[...]
    h = _rms_bf16(x, g_ref[...])
    gates = jax.nn.sigmoid(_dot(h, wg_ref[...]) + bg_ref[...])
    merged = (gates[:, :D_MODEL] * _dot(oa_ref[...], wa_ref[...])
              + gates[:, D_MODEL:2 * D_MODEL] * _dot(ob_ref[...], wb_ref[...])
              + gates[:, 2 * D_MODEL:] * _dot(oc_ref[...], wc_ref[...]))
    o_ref[...] = x + _dot(merged.astype(BF16), wo_ref[...])


def _merge(x, oa, ob, oc, g, wg, bg, wa, wb, wc, wo, *, layer, tm=512):
    n, d = x.shape
    rows = lambda w: pl.BlockSpec((tm, w), lambda i: (i, 0))
    consts = [g, wg, bg, wa, wb, wc, wo]
    return pl.pallas_call(
        _merge_kernel,
        grid=(n // tm,),
        in_specs=[rows(d), rows(oa.shape[1]), rows(ob.shape[1]), rows(oc.shape[1])]
                 + [_layer_spec(c, layer) if c.ndim == 3 else _const_spec(c.shape) for c in consts],
        out_specs=rows(d),
        out_shape=jax.ShapeDtypeStruct((n, d), F32),
        compiler_params=_params(("parallel",)),
        name="merge",
    )(x, oa, ob, oc, *consts)


def _rel_generator_index():
    j = np.arange(REL_VEC)
    dist = np.where(j <= PAIR_KEYS, LEFT_CHUNKS * CHUNK - j, MAX_REL)
    return np.clip(dist, -MAX_REL, MAX_REL) + MAX_REL


def _cast_kernel(w_ref, o_ref):
    o_ref[...] = w_ref[...].astype(BF16)


def _cast_bf16(w, rows):
    l, r, c = w.shape
    blk = pl.BlockSpec((1, rows, c), lambda li, ri: (li, ri, 0))
    return pl.pallas_call(
        _cast_kernel, grid=(l, r // rows), in_specs=[blk], out_specs=blk,
        out_shape=jax.ShapeDtypeStruct(w.shape, BF16),
        compiler_params=_params(("parallel", "parallel")), name="cast",
    )(w)


def _split_w_in_kernel(w_ref, scale_ref, qkv_ref, f_ref, g_ref):
    w = w_ref[0]
    nf = N_HEADS_FOX
    qkv_ref[0] = (w[:, :QKV_WIDTH] * scale_ref[...]).astype(BF16)
    slab = w[:, QKV_WIDTH:QKV_WIDTH + LANES]
    lane = lax.broadcasted_iota(jnp.int32, slab.shape, 1)
    f_ref[0] = jnp.where(lane < nf, slab, 0.0).astype(BF16)
    g_ref[0] = w[:, QKV_WIDTH + nf:].astype(BF16)


def _split_w_in(w_in, col_scale, rows=256):
    l, d, width = w_in.shape
    ng = width - QKV_WIDTH - N_HEADS_FOX
    out = lambda c: pl.BlockSpec((1, rows, c), lambda li, ri: (li, ri, 0))
    return pl.pallas_call(
        _split_w_in_kernel, grid=(l, d // rows),
        in_specs=[out(width), pl.BlockSpec((1, QKV_WIDTH), lambda li, ri: (0, 0))],
        out_specs=[out(QKV_WIDTH), out(LANES), out(ng)],
        out_shape=[jax.ShapeDtypeStruct((l, d, QKV_WIDTH), BF16), jax.ShapeDtypeStruct((l, d, LANES), BF16),
                   jax.ShapeDtypeStruct((l, d, ng), BF16)],
        compiler_params=_params(("parallel", "parallel")), name="split_w_in",
    )(w_in, col_scale)


def _q_col_scale():
    col_scale = np.ones((1, QKV_WIDTH), np.float32)
    for start, width in ((0, W_SB), (3 * W_SB, W_CH), (3 * W_SB + 3 * W_CH, W_FOX)):
        col_scale[:, start:start + width] = HEAD_DIM ** -0.5 * LOG2E
    return jnp.asarray(col_scale)


def kernel(x, g_ffn1, w_ffn1_in, w_ffn1_out, g_mix, w_in, b_in, rel_bias, w_br_sb, w_br_ch, w_br_fox, w_out,
           g_ffn2, w_ffn2_in, w_ffn2_out, g_final):
    batch, seq, d = x.shape
    depth = g_ffn1.shape[0]
    nf = N_HEADS_FOX
    col_scale = _q_col_scale()
    ffn1_in, ffn2_in = _cast_bf16(w_ffn1_in, 256), _cast_bf16(w_ffn2_in, 256)
    ffn1_out, ffn2_out = _cast_bf16(w_ffn1_out, 704), _cast_bf16(w_ffn2_out, 704)
    br_sb, br_ch, br_fox = _cast_bf16(w_br_sb, W_SB), _cast_bf16(w_br_ch, W_CH), _cast_bf16(w_br_fox, W_FOX)
    wo = _cast_bf16(w_out, d)
    w_qkv, w_f, w_g = _split_w_in(w_in, col_scale)
    b_qkv = b_in[:, :QKV_WIDTH] * col_scale
    b_f = jnp.pad(b_in[:, QKV_WIDTH:QKV_WIDTH + nf], ((0, 0), (0, LANES - nf)))
    b_g = b_in[:, QKV_WIDTH + nf:]
    rel_gen = (rel_bias[:, _rel_generator_index(), :] * LOG2E).transpose(0, 2, 1).reshape(
        depth, N_HEADS_CH // 2, 2, REL_VEC)
    xf = x.reshape(batch * seq, d)
    gf = g_final[None, :]
    for l in range(depth):
        xf = _ffn(xf, g_ffn1[l][None, :], ffn1_in, ffn1_out, gf, layer=l, final_norm=False)
        qkv, fcum = _inproj(xf, g_mix[l][None, :], w_qkv, b_qkv[l][None, :], w_f, b_f[l][None, :],
                            layer=l, batch=batch)
        qkv = qkv.reshape(batch, seq, QKV_WIDTH)
        o_a = _sb_attention(qkv)
        o_b = _chunk_attention(qkv, rel_gen[l])
        o_c = _fox_attention(qkv, fcum[:, :nf].reshape(batch, nf, 1, seq))
        xf = _merge(xf, o_a.reshape(batch * seq, W_SB), o_b.reshape(batch * seq, W_CH),
                    o_c.reshape(batch * seq, W_FOX), g_mix[l][None, :], w_g, b_g[l][None, :],
                    br_sb, br_ch, br_fox, wo, layer=l)
        xf = _ffn(xf, g_ffn2[l][None, :], ffn2_in, ffn2_out, gf, layer=l, final_norm=(l == depth - 1))
    return xf.reshape(batch, seq, d)
```

```python
import functools

import numpy as np
import jax
import jax.numpy as jnp
from jax import lax
from jax.experimental import pallas as pl
from jax.experimental.pallas import tpu as pltpu

D_MODEL = 1024
HEAD_DIM = 64
CHUNK = 64
LEFT_CHUNKS = 8
MAX_REL = 128
N_HEADS_SB = 4
N_HEADS_CH = 8
N_HEADS_FOX = 4
W_SB = N_HEADS_SB * HEAD_DIM
W_CH = N_HEADS_CH * HEAD_DIM
W_FOX = N_HEADS_FOX * HEAD_DIM
QKV_WIDTH = 3 * (W_SB + W_CH + W_FOX)
RMS_EPS = 1e-6
NEG = -1e30
LOG2E = 1.4426950408889634

LANES = 128
PAIR_Q = 2 * CHUNK
PAIR_KEYS = PAIR_Q + LEFT_CHUNKS * CHUNK
REL_VEC = 768
VMEM_LIMIT = 56 * 1024 * 1024

F32 = jnp.float32
BF16 = jnp.bfloat16


def _rms_bf16(x, g):
    ms = jnp.mean(x * x, axis=-1, keepdims=True)
    return ((x * lax.rsqrt(ms + RMS_EPS)) * g).astype(BF16)


def _dot(a, b):
    return jnp.dot(a, b, preferred_element_type=F32)


def _dot_nt(a, b):
    return lax.dot_general(a, b, (((1,), (1,)), ((), ())), preferred_element_type=F32)


def _const_spec(shape):
    nd = len(shape)
    return pl.BlockSpec(shape, lambda *_: (0,) * nd, pipeline_mode=pl.Buffered(1))


def _layer_spec(w, layer, col_blocks=1, col=0):
    _, r, c = w.shape
    return pl.BlockSpec((None, r, c // col_blocks), lambda *_: (layer, 0, col), pipeline_mode=pl.Buffered(1))


def _params(sem):
    return pltpu.CompilerParams(dimension_semantics=sem, vmem_limit_bytes=VMEM_LIMIT)


def _ffn_kernel(x_ref, g_ref, wg_ref, wu_ref, wo_ref, gf_ref, o_ref, *, final_norm):
    x = x_ref[...]
    h = _rms_bf16(x, g_ref[...])
    gate = _dot(h, wg_ref[...])
    up = _dot(h, wu_ref[...])
    a = (gate * jax.nn.sigmoid(gate) * up).astype(BF16)
    y = x + 0.5 * _dot(a, wo_ref[...])
    if final_norm:
        ms = jnp.mean(y * y, axis=-1, keepdims=True)
        y = (y * lax.rsqrt(ms + RMS_EPS)) * gf_ref[...]
    o_ref[...] = y


def _ffn(x, g, w_in, w_out, gf, *, layer, final_norm, tm=512):
    n, d = x.shape
    row = pl.BlockSpec((tm, d), lambda i: (i, 0))
    return pl.pallas_call(
        functools.partial(_ffn_kernel, final_norm=final_norm),
        grid=(n // tm,),
        in_specs=[row, _const_spec(g.shape), _layer_spec(w_in, layer, 2, 0), _layer_spec(w_in, layer, 2, 1),
                  _layer_spec(w_out, layer), _const_spec(gf.shape)],
        out_specs=row,
        out_shape=jax.ShapeDtypeStruct((n, d), F32),
        compiler_params=_params(("parallel",)),
        name="ffn",
    )(x, g, w_in, w_in, w_out, gf)


def _inproj_kernel(x_ref, g_ref, w_ref, b_ref, wf_ref, bf_ref, qkv_ref, fc_ref, carry_ref, *, tm):
    @pl.when(pl.program_id(1) == 0)
    def _():
        carry_ref[...] = jnp.zeros_like(carry_ref)

    h = _rms_bf16(x_ref[...], g_ref[...])
    f = _dot(h, wf_ref[...]) + bf_ref[...]
    lf = (jnp.minimum(f, 0.0) - jnp.log1p(jnp.exp(-jnp.abs(f)))) * LOG2E
    s = lf.T[:8, :]
    lane = lax.broadcasted_iota(jnp.int32, s.shape, 1)
    sh = 1
    while sh < tm:
        s = s + jnp.where(lane >= sh, pltpu.roll(s, sh, axis=1), 0.0)
        sh *= 2
    fc = s + carry_ref[:, 0:1]
    fc_ref[0] = fc
    carry_ref[...] = jnp.broadcast_to(fc[:, tm - 1:tm], carry_ref.shape)
    qkv_ref[...] = (_dot(h, w_ref[...]) + b_ref[...]).astype(BF16)


def _inproj(x, g, w, b, wf, bfg, *, layer, batch, tm=512):
    n, d = x.shape
    seq = n // batch
    nt = seq // tm
    return pl.pallas_call(
        functools.partial(_inproj_kernel, tm=tm),
        grid=(batch, nt),
        in_specs=[pl.BlockSpec((tm, d), lambda bi, ti: (bi * nt + ti, 0)),
                  _const_spec(g.shape), _layer_spec(w, layer), _const_spec(b.shape),
                  _layer_spec(wf, layer), _const_spec(bfg.shape)],
        out_specs=[pl.BlockSpec((tm, QKV_WIDTH), lambda bi, ti: (bi * nt + ti, 0)),
                   pl.BlockSpec((1, 8, tm), lambda bi, ti: (bi, 0, ti))],
        out_shape=[jax.ShapeDtypeStruct((n, QKV_WIDTH), BF16),
                   jax.ShapeDtypeStruct((batch, 8, seq), F32)],
        scratch_shapes=[pltpu.VMEM((8, LANES), F32)],
        compiler_params=_params(("parallel", "arbitrary")),
        name="inproj",
    )(x, g, w, b, wf, bfg)


def _head_masks(shape):
    lane = lax.broadcasted_iota(jnp.int32, shape, 1)
    lo = lane < HEAD_DIM
    return lo, jnp.logical_not(lo)


def _tile_schedule(n_tiles, diag_first):
    steps = []
    for i in range(n_tiles):
        kbs = range(i, -1, -1) if diag_first else range(i + 1)
        steps += [(i, kb, kb == i) for kb in kbs]
    return steps


def _run_pipeline(n_steps, stage0, stage1, stage2):
    for t in range(n_steps + 2):
        if t < n_steps:
            stage0(t, t % 2)
        if 1 <= t <= n_steps:
            stage1(t - 1, (t - 1) % 2)
        if t >= 2:
            stage2(t - 2, t % 2)


def _sb_kernel(q_ref, k_ref, v_ref, o_ref, mask_ref, tri_ref,
               zs00, zs01, zs10, zs11, sp00, sp01, sp10, sp11, c00, c01, c10, c11, w00, w01, w10, w11,
               carry0, carry1, acc0, acc1, *, tq, steps):
    zs_refs, sp_refs = ((zs00, zs01), (zs10, zs11)), ((sp00, sp01), (sp10, sp11))
    c_refs, w_refs = ((c00, c01), (c10, c11)), ((w00, w01), (w10, w11))
    carry_refs, acc_refs = (carry0, carry1), (acc0, acc1)
    row = lax.broadcasted_iota(jnp.int32, (tq, tq), 0)
    col = lax.broadcasted_iota(jnp.int32, (tq, tq), 1)
    mask_ref[...] = jnp.where(col < row, 0.0, NEG)
    tri_ref[...] = jnp.where(row > col, 1.0, 0.0).astype(BF16)
    masks = _head_masks((tq, LANES))
    for e in range(2):
        carry_refs[e][...] = jnp.zeros(carry_refs[e].shape, F32)
        acc_refs[e][...] = jnp.zeros(acc_refs[e].shape, F32)

    def logits(n, slot):
        i, kb, diag = steps[n]
        rq = pl.ds(i * tq, tq)
        q = q_ref[0, rq, :]
        k = k_ref[0, pl.ds(kb * tq, tq), :]
        for e in range(2):
            qm = jnp.where(masks[e], q, jnp.zeros_like(q))
            z = _dot_nt(qm, k)
            if diag:
                z = z + mask_ref[...]
            sp = jnp.maximum(z, 0.0) + jnp.log2(1.0 + jnp.exp2(-jnp.abs(z)))
            zs_refs[slot][e][...] = z - sp
            sp_refs[slot][e][...] = sp.astype(BF16)
            carry = carry_refs[e][rq, :]
            c_refs[slot][e][...] = carry
            carry_refs[e][rq, :] = carry + jnp.sum(sp, axis=1, keepdims=True)

    def weights(n, slot):
        for e in range(2):
            right = _dot(sp_refs[slot][e][...], tri_ref[...])
            right = right + jnp.concatenate([c_refs[slot][e][...]] * (tq // LANES), axis=1)
            w_refs[slot][e][...] = jnp.exp2(zs_refs[slot][e][...] - right).astype(BF16)

    def pv(n, slot):
        i, kb, _ = steps[n]
        rq = pl.ds(i * tq, tq)
        v = v_ref[0, pl.ds(kb * tq, tq), :]
        for e in range(2):
            acc_refs[e][rq, :] = acc_refs[e][rq, :] + _dot(w_refs[slot][e][...], v)

    _run_pipeline(len(steps), logits, weights, pv)
    full = _head_masks(acc_refs[0].shape)
    o_ref[0] = jnp.where(full[0], acc_refs[0][...], acc_refs[1][...]).astype(BF16)


def _sb_attention(qkv, *, tq=256):
    batch, seq, _ = qkv.shape
    npair = W_SB // LANES
    steps = _tile_schedule(seq // tq, diag_first=True)
    blk = lambda off: pl.BlockSpec((1, seq, LANES), lambda b, p: (b, 0, off + p))
    zs_shape, sp_shape = pltpu.VMEM((tq, tq), F32), pltpu.VMEM((tq, tq), BF16)
    c_shape, w_shape = pltpu.VMEM((tq, LANES), F32), pltpu.VMEM((tq, tq), BF16)
    acc_shape = pltpu.VMEM((seq, LANES), F32)
    return pl.pallas_call(
        functools.partial(_sb_kernel, tq=tq, steps=steps),
        grid=(batch, npair),
        in_specs=[blk(0), blk(npair), blk(2 * npair)],
        out_specs=pl.BlockSpec((1, seq, LANES), lambda b, p: (b, 0, p)),
        out_shape=jax.ShapeDtypeStruct((batch, seq, W_SB), BF16),
        scratch_shapes=[pltpu.VMEM((tq, tq), F32), pltpu.VMEM((tq, tq), BF16)]
                       + [zs_shape] * 4 + [sp_shape] * 4 + [c_shape] * 4 + [w_shape] * 4 + [acc_shape] * 4,
        compiler_params=_params(("parallel", "parallel")),
        name="sb_attn",
    )(qkv, qkv, qkv)


def _fox_kernel(q_ref, k_ref, v_ref, f_ref, o_ref, mask_ref,
                s00, s01, s10, s11, p00, p01, p10, p11, a00, a01, a10, a11,
                m0, m1, acc0, acc1, va0, va1, *, tq, steps):
    s_refs, p_refs, a_refs = ((s00, s01), (s10, s11)), ((p00, p01), (p10, p11)), ((a00, a01), (a10, a11))
    m_refs, acc_refs, va_refs = (m0, m1), (acc0, acc1), (va0, va1)
    row = lax.broadcasted_iota(jnp.int32, (tq, tq), 0)
    col = lax.broadcasted_iota(jnp.int32, (tq, tq), 1)
    mask_ref[...] = jnp.where(col <= row, 0.0, NEG)
    masks = _head_masks((tq, LANES))
    v_all = v_ref[0]
    full = _head_masks(v_all.shape)
    for e in range(2):
        m_refs[e][...] = jnp.full(m_refs[e].shape, NEG, F32)
        acc_refs[e][...] = jnp.zeros(acc_refs[e].shape, F32)
        va_refs[e][...] = jnp.where(full[e], v_all, jnp.ones_like(v_all))

    def qk(n, slot):
        i, kb, _ = steps[n]
        q = q_ref[0, pl.ds(i * tq, tq), :]
        k = k_ref[0, pl.ds(kb * tq, tq), :]
        for e in range(2):
            qm = jnp.where(masks[e], q, jnp.zeros_like(q))
            s_refs[slot][e][...] = _dot_nt(qm, k)

    def sm(n, slot):
        i, kb, diag = steps[n]
        rq, rk = pl.ds(i * tq, tq), pl.ds(kb * tq, tq)
        for e in range(2):
            fq0 = f_ref[e, :, rq][:, 0:1]
            s = s_refs[slot][e][...] + (fq0 - f_ref[e, :, rk])
            if diag:
                s = s + mask_ref[...]
            m_prev = m_refs[e][rq, :]
            m_new = jnp.maximum(m_prev, jnp.max(s, axis=1, keepdims=True))
            m_refs[e][rq, :] = m_new
            a_refs[slot][e][...] = jnp.exp2(m_prev - m_new)
            p_refs[slot][e][...] = jnp.exp2(s - jnp.concatenate([m_new] * (tq // LANES), axis=1)).astype(BF16)

    def pv(n, slot):
        i, kb, _ = steps[n]
        rq, rk = pl.ds(i * tq, tq), pl.ds(kb * tq, tq)
        for e in range(2):
            acc_refs[e][rq, :] = (a_refs[slot][e][...] * acc_refs[e][rq, :]
                                  + _dot(p_refs[slot][e][...], va_refs[e][rk, :]))

    _run_pipeline(len(steps), qk, sm, pv)
    outs = []
    for e in range(2):
        acc = acc_refs[e][...]
        outs.append(acc * (1.0 / pltpu.roll(acc, HEAD_DIM, axis=1)))
    o_ref[0] = jnp.where(full[0], outs[0], outs[1]).astype(BF16)


def _fox_attention(qkv, fcum, *, tq=256):
    batch, seq, _ = qkv.shape
    cb = (3 * W_SB + 3 * W_CH) // LANES
    npair = W_FOX // LANES
    steps = _tile_schedule(seq // tq, diag_first=False)
    blk = lambda off: pl.BlockSpec((1, seq, LANES), lambda b, p: (b, 0, cb + off + p))
    s_shape, p_shape, a_shape = pltpu.VMEM((tq, tq), F32), pltpu.VMEM((tq, tq), BF16), pltpu.VMEM((tq, LANES), F32)
    acc_shape, v_shape = pltpu.VMEM((seq, LANES), F32), pltpu.VMEM((seq, LANES), BF16)
    return pl.pallas_call(
        functools.partial(_fox_kernel, tq=tq, steps=steps),
        grid=(batch, npair),
        in_specs=[blk(0), blk(npair), blk(2 * npair),
                  pl.BlockSpec((None, 2, 1, seq), lambda b, p: (b, p, 0, 0))],
        out_specs=pl.BlockSpec((1, seq, LANES), lambda b, p: (b, 0, p)),
        out_shape=jax.ShapeDtypeStruct((batch, seq, W_FOX), BF16),
        scratch_shapes=[pltpu.VMEM((tq, tq), F32)] + [s_shape] * 4 + [p_shape] * 4 + [a_shape] * 4
                       + [acc_shape] * 4 + [v_shape] * 2,
        compiler_params=_params(("parallel", "parallel")),
        name="fox_attn",
    )(qkv, qkv, qkv, fcum)


def _chunk_kernel(q_ref, k_ref, v_ref, r_ref, o_ref, bias_ref, s0_ref, s1_ref, p0_ref, p1_ref, l0_ref, l1_ref,
                  *, seq):
    n_tiles = seq // PAIR_Q
    rows = lax.broadcasted_iota(jnp.int32, (PAIR_Q, PAIR_KEYS), 0)
    cols = lax.broadcasted_iota(jnp.int32, (PAIR_Q, PAIR_KEYS), 1)
    first_key = jnp.where(rows < CHUNK, 0, CHUNK)
    valid = (cols >= first_key) & (cols < first_key + PAIR_KEYS - CHUNK)
    masks = _head_masks((PAIR_Q, LANES))
    for e in range(2):
        gen = jnp.broadcast_to(r_ref[0, e:e + 1, :], (PAIR_Q, REL_VEC))
        toeplitz = pltpu.roll(gen, 0, 1, stride=1, stride_axis=0)
        bias_ref[e] = jnp.where(valid, toeplitz[:, :PAIR_KEYS], NEG)
    s_refs, p_refs, l_refs = (s0_ref, s1_ref), (p0_ref, p1_ref), (l0_ref, l1_ref)

    def window(t):
        nk = min(PAIR_KEYS, PAIR_Q * (t + 1))
        return pl.ds(PAIR_Q * (t + 1) - nk, nk), nk

    def qk(t, slot):
        keys, nk = window(t)
        q = q_ref[0, pl.ds(t * PAIR_Q, PAIR_Q), :]
        k = k_ref[0, keys, :]
        for e in range(2):
            qm = jnp.where(masks[e], q, jnp.zeros_like(q))
            s_refs[slot][e, :, :nk] = _dot_nt(qm, k)

    def sm(t, slot):
        _, nk = window(t)
        for e in range(2):
            s = s_refs[slot][e, :, :nk] + bias_ref[e, :, PAIR_KEYS - nk:]
            p = jnp.exp2(s - jnp.max(s, axis=1, keepdims=True))
            l_refs[slot][e] = jnp.sum(p, axis=1, keepdims=True)
            p_refs[slot][e, :, :nk] = p.astype(BF16)

    def pv(t, slot):
        keys, nk = window(t)
        v = v_ref[0, keys, :]
        outs = [_dot(p_refs[slot][e, :, :nk], v) * (1.0 / l_refs[slot][e]) for e in range(2)]
        o_ref[0, pl.ds(t * PAIR_Q, PAIR_Q), :] = jnp.where(masks[0], outs[0], outs[1]).astype(BF16)

    _run_pipeline(n_tiles, qk, sm, pv)


def _chunk_attention(qkv, rel_gen):
    batch, seq, _ = qkv.shape
    cb = 3 * W_SB // LANES
    npair = W_CH // LANES
    blk = lambda off: pl.BlockSpec((1, seq, LANES), lambda b, p: (b, 0, cb + off + p))
    s_shape = pltpu.VMEM((2, PAIR_Q, PAIR_KEYS), F32)
    p_shape = pltpu.VMEM((2, PAIR_Q, PAIR_KEYS), BF16)
    l_shape = pltpu.VMEM((2, PAIR_Q, 1), F32)
    return pl.pallas_call(
        functools.partial(_chunk_kernel, seq=seq),
        grid=(batch, npair),
        in_specs=[blk(0), blk(npair), blk(2 * npair),
                  pl.BlockSpec((1, 2, REL_VEC), lambda b, p: (p, 0, 0))],
        out_specs=pl.BlockSpec((1, seq, LANES), lambda b, p: (b, 0, p)),
        out_shape=jax.ShapeDtypeStruct((batch, seq, W_CH), BF16),
        scratch_shapes=[pltpu.VMEM((2, PAIR_Q, PAIR_KEYS), F32),
                        s_shape, s_shape, p_shape, p_shape, l_shape, l_shape],
        compiler_params=_params(("parallel", "parallel")),
        name="chunk_attn",
    )(qkv, qkv, qkv, rel_gen)


def _merge_kernel(x_ref, oa_ref, ob_ref, oc_ref, g_ref, wg_ref, bg_ref, wa_ref, wb_ref, wc_ref, wo_ref, o_ref):
    x = x_ref[...]
    h = _rms_bf16(x, g_ref[...])
    gates = jax.nn.sigmoid(_dot(h, wg_ref[...]) + bg_ref[...])
    merged = (gates[:, :D_MODEL] * _dot(oa_ref[...], wa_ref[...])
              + gates[:, D_MODEL:2 * D_MODEL] * _dot(ob_ref[...], wb_ref[...])
              + gates[:, 2 * D_MODEL:] * _dot(oc_ref[...], wc_ref[...]))
    o_ref[...] = x + _dot(merged.astype(BF16), wo_ref[...])


def _merge(x, oa, ob, oc, g, wg, bg, wa, wb, wc, wo, *, layer, tm=512):
    n, d = x.shape
    rows = lambda w: pl.BlockSpec((tm, w), lambda i: (i, 0))
    consts = [g, wg, bg, wa, wb, wc, wo]
    return pl.pallas_call(
        _merge_kernel,
        grid=(n // tm,),
        in_specs=[rows(d), rows(oa.shape[1]), rows(ob.shape[1]), rows(oc.shape[1])]
                 + [_layer_spec(c, layer) if c.ndim == 3 else _const_spec(c.shape) for c in consts],
        out_specs=rows(d),
        out_shape=jax.ShapeDtypeStruct((n, d), F32),
        compiler_params=_params(("parallel",)),
        name="merge",
    )(x, oa, ob, oc, *consts)


def _rel_generator_index():
    j = np.arange(REL_VEC)
    dist = np.where(j <= PAIR_KEYS, LEFT_CHUNKS * CHUNK - j, MAX_REL)
    return np.clip(dist, -MAX_REL, MAX_REL) + MAX_REL


def _cast_kernel(w_ref, o_ref):
    o_ref[...] = w_ref[...].astype(BF16)


def _cast_bf16(w, rows):
    l, r, c = w.shape
    blk = pl.BlockSpec((1, rows, c), lambda li, ri: (li, ri, 0))
    return pl.pallas_call(
        _cast_kernel, grid=(l, r // rows), in_specs=[blk], out_specs=blk,
        out_shape=jax.ShapeDtypeStruct(w.shape, BF16),
        compiler_params=_params(("parallel", "parallel")), name="cast",
    )(w)


def _split_w_in_kernel(w_ref, qkv_ref, f_ref, g_ref):
    w = w_ref[0]
    nf = N_HEADS_FOX
    qkv_ref[0] = w[:, :QKV_WIDTH]
    slab = w[:, QKV_WIDTH:QKV_WIDTH + LANES]
    lane = lax.broadcasted_iota(jnp.int32, slab.shape, 1)
    f_ref[0] = jnp.where(lane < nf, slab, jnp.zeros_like(slab))
    g_ref[0] = w[:, QKV_WIDTH + nf:]


def _split_w_in(w_in, rows=256):
    l, d, width = w_in.shape
    ng = width - QKV_WIDTH - N_HEADS_FOX
    out = lambda c: pl.BlockSpec((1, rows, c), lambda li, ri: (li, ri, 0))
    return pl.pallas_call(
        _split_w_in_kernel, grid=(l, d // rows),
        in_specs=[out(width)],
        out_specs=[out(QKV_WIDTH), out(LANES), out(ng)],
        out_shape=[jax.ShapeDtypeStruct((l, d, QKV_WIDTH), BF16), jax.ShapeDtypeStruct((l, d, LANES), BF16),
                   jax.ShapeDtypeStruct((l, d, ng), BF16)],
        compiler_params=_params(("parallel", "parallel")), name="split_w_in",
    )(w_in)


def _q_col_scale(width):
    col_scale = np.ones((1, width), np.float32)
    for start, w in ((0, W_SB), (3 * W_SB, W_CH), (3 * W_SB + 3 * W_CH, W_FOX)):
        col_scale[:, start:start + w] = HEAD_DIM ** -0.5 * LOG2E
    return jnp.asarray(col_scale)


def kernel(x, g_ffn1, w_ffn1_in, w_ffn1_out, g_mix, w_in, b_in, rel_bias, w_br_sb, w_br_ch, w_br_fox, w_out,
           g_ffn2, w_ffn2_in, w_ffn2_out, g_final):
    batch, seq, d = x.shape
    depth = g_ffn1.shape[0]
    nf = N_HEADS_FOX
    col_scale = _q_col_scale(w_in.shape[-1])
    ffn1_in, ffn2_in = _cast_bf16(w_ffn1_in, 256), _cast_bf16(w_ffn2_in, 256)
    ffn1_out, ffn2_out = _cast_bf16(w_ffn1_out, 704), _cast_bf16(w_ffn2_out, 704)
    br_sb, br_ch, br_fox = _cast_bf16(w_br_sb, W_SB), _cast_bf16(w_br_ch, W_CH), _cast_bf16(w_br_fox, W_FOX)
    wo = _cast_bf16(w_out, d)
    w_qkv, w_f, w_g = _split_w_in((w_in * col_scale).astype(BF16))
    b_s = b_in * col_scale
    b_qkv = b_s[:, :QKV_WIDTH]
    b_f = jnp.pad(b_s[:, QKV_WIDTH:QKV_WIDTH + nf], ((0, 0), (0, LANES - nf)))
    b_g = b_s[:, QKV_WIDTH + nf:]
    rel_gen = (rel_bias[:, _rel_generator_index(), :] * LOG2E).transpose(0, 2, 1).reshape(
        depth, N_HEADS_CH // 2, 2, REL_VEC)
    xf = x.reshape(batch * seq, d)
    gf = g_final[None, :]
    for l in range(depth):
        xf = _ffn(xf, g_ffn1[l][None, :], ffn1_in, ffn1_out, gf, layer=l, final_norm=False)
        qkv, fcum = _inproj(xf, g_mix[l][None, :], w_qkv, b_qkv[l][None, :], w_f, b_f[l][None, :],
                            layer=l, batch=batch)
        qkv = qkv.reshape(batch, seq, QKV_WIDTH)
        o_a = _sb_attention(qkv)
        o_b = _chunk_attention(qkv, rel_gen[l])
        o_c = _fox_attention(qkv, fcum[:, :nf].reshape(batch, nf, 1, seq))
        xf = _merge(xf, o_a.reshape(batch * seq, W_SB), o_b.reshape(batch * seq, W_CH),
                    o_c.reshape(batch * seq, W_FOX), g_mix[l][None, :], w_g, b_g[l][None, :],
                    br_sb, br_ch, br_fox, wo, layer=l)
        xf = _ffn(xf, g_ffn2[l][None, :], ffn2_in, ffn2_out, gf, layer=l, final_norm=(l == depth - 1))
    return xf.reshape(batch, seq, d)
```

```python
import functools

import numpy as np
import jax
import jax.numpy as jnp
from jax import lax
from jax.experimental import pallas as pl
from jax.experimental.pallas import tpu as pltpu

D_MODEL = 1024
HEAD_DIM = 64
CHUNK = 64
LEFT_CHUNKS = 8
MAX_REL = 128
N_HEADS_SB = 4
N_HEADS_CH = 8
N_HEADS_FOX = 4
W_SB = N_HEADS_SB * HEAD_DIM
W_CH = N_HEADS_CH * HEAD_DIM
W_FOX = N_HEADS_FOX * HEAD_DIM
QKV_WIDTH = 3 * (W_SB + W_CH + W_FOX)
RMS_EPS = 1e-6
NEG = -1e30
LOG2E = 1.4426950408889634

LANES = 128
PAIR_Q = 2 * CHUNK
PAIR_KEYS = PAIR_Q + LEFT_CHUNKS * CHUNK
REL_VEC = 768
VMEM_LIMIT = 56 * 1024 * 1024

F32 = jnp.float32
BF16 = jnp.bfloat16


def _rms_bf16(x, g):
    ms = jnp.mean(x * x, axis=-1, keepdims=True)
    return ((x * lax.rsqrt(ms + RMS_EPS)) * g).astype(BF16)


def _dot(a, b):
    return jnp.dot(a, b, preferred_element_type=F32)


def _dot_nt(a, b):
    return lax.dot_general(a, b, (((1,), (1,)), ((), ())), preferred_element_type=F32)


def _const_spec(shape):
    nd = len(shape)
    return pl.BlockSpec(shape, lambda *_: (0,) * nd, pipeline_mode=pl.Buffered(1))


def _layer_spec(w, layer, col_blocks=1, col=0):
    _, r, c = w.shape
    return pl.BlockSpec((None, r, c // col_blocks), lambda *_: (layer, 0, col), pipeline_mode=pl.Buffered(1))


def _params(sem):
    return pltpu.CompilerParams(dimension_semantics=sem, vmem_limit_bytes=VMEM_LIMIT)


def _ffn_kernel(x_ref, g_ref, wg_ref, wu_ref, wo_ref, gf_ref, o_ref, *, final_norm):
    x = x_ref[...]
    h = _rms_bf16(x, g_ref[...])
    gate = _dot(h, wg_ref[...])
    up = _dot(h, wu_ref[...])
    a = (gate * jax.nn.sigmoid(gate) * up).astype(BF16)
    y = x + 0.5 * _dot(a, wo_ref[...])
    if final_norm:
        ms = jnp.mean(y * y, axis=-1, keepdims=True)
        y = (y * lax.rsqrt(ms + RMS_EPS)) * gf_ref[...]
    o_ref[...] = y


def _ffn(x, g, w_in, w_out, gf, *, layer, final_norm, tm=512):
    n, d = x.shape
    row = pl.BlockSpec((tm, d), lambda i: (i, 0))
    return pl.pallas_call(
        functools.partial(_ffn_kernel, final_norm=final_norm),
        grid=(n // tm,),
        in_specs=[row, _const_spec(g.shape), _layer_spec(w_in, layer, 2, 0), _layer_spec(w_in, layer, 2, 1),
                  _layer_spec(w_out, layer), _const_spec(gf.shape)],
        out_specs=row,
        out_shape=jax.ShapeDtypeStruct((n, d), F32),
        compiler_params=_params(("parallel",)),
        name="ffn",
    )(x, g, w_in, w_in, w_out, gf)


def _inproj_kernel(x_ref, g_ref, w_ref, b_ref, wf_ref, bf_ref, qkv_ref, fc_ref, carry_ref, *, tm):
    @pl.when(pl.program_id(1) == 0)
    def _():
        carry_ref[...] = jnp.zeros_like(carry_ref)

    h = _rms_bf16(x_ref[...], g_ref[...])
    f = _dot(h, wf_ref[...]) + bf_ref[...]
    lf = (jnp.minimum(f, 0.0) - jnp.log1p(jnp.exp(-jnp.abs(f)))) * LOG2E
    s = lf.T[:8, :]
    lane = lax.broadcasted_iota(jnp.int32, s.shape, 1)
    sh = 1
    while sh < tm:
        s = s + jnp.where(lane >= sh, pltpu.roll(s, sh, axis=1), 0.0)
        sh *= 2
    fc = s + carry_ref[:, 0:1]
    fc_ref[0] = fc
    carry_ref[...] = jnp.broadcast_to(fc[:, tm - 1:tm], carry_ref.shape)
    qkv_ref[...] = (_dot(h, w_ref[...]) + b_ref[...]).astype(BF16)


def _inproj(x, g, w, b, wf, bfg, *, layer, batch, tm=512):
    n, d = x.shape
    seq = n // batch
    nt = seq // tm
    return pl.pallas_call(
        functools.partial(_inproj_kernel, tm=tm),
        grid=(batch, nt),
        in_specs=[pl.BlockSpec((tm, d), lambda bi, ti: (bi * nt + ti, 0)),
                  _const_spec(g.shape), _layer_spec(w, layer), _const_spec(b.shape),
                  _layer_spec(wf, layer), _const_spec(bfg.shape)],
        out_specs=[pl.BlockSpec((tm, QKV_WIDTH), lambda bi, ti: (bi * nt + ti, 0)),
                   pl.BlockSpec((1, 8, tm), lambda bi, ti: (bi, 0, ti))],
        out_shape=[jax.ShapeDtypeStruct((n, QKV_WIDTH), BF16),
                   jax.ShapeDtypeStruct((batch, 8, seq), F32)],
        scratch_shapes=[pltpu.VMEM((8, LANES), F32)],
        compiler_params=_params(("parallel", "arbitrary")),
        name="inproj",
    )(x, g, w, b, wf, bfg)


def _head_masks(shape):
    lane = lax.broadcasted_iota(jnp.int32, shape, 1)
    lo = lane < HEAD_DIM
    return lo, jnp.logical_not(lo)


def _tile_schedule(n_tiles, diag_first):
    steps = []
    for i in range(n_tiles):
        kbs = range(i, -1, -1) if diag_first else range(i + 1)
        steps += [(i, kb, kb == i) for kb in kbs]
    return steps


def _run_pipeline(n_steps, stage0, stage1, stage2):
    for t in range(n_steps + 2):
        if t < n_steps:
            stage0(t, t % 2)
        if 1 <= t <= n_steps:
            stage1(t - 1, (t - 1) % 2)
        if t >= 2:
            stage2(t - 2, t % 2)


def _sb_kernel(q_ref, k_ref, v_ref, o_ref, mask_ref, tri_ref,
               zs00, zs01, zs10, zs11, sp00, sp01, sp10, sp11, c00, c01, c10, c11, w00, w01, w10, w11,
               carry0, carry1, acc0, acc1, *, tq, steps):
    zs_refs, sp_refs = ((zs00, zs01), (zs10, zs11)), ((sp00, sp01), (sp10, sp11))
    c_refs, w_refs = ((c00, c01), (c10, c11)), ((w00, w01), (w10, w11))
    carry_refs, acc_refs = (carry0, carry1), (acc0, acc1)
    row = lax.broadcasted_iota(jnp.int32, (tq, tq), 0)
    col = lax.broadcasted_iota(jnp.int32, (tq, tq), 1)
    mask_ref[...] = jnp.where(col < row, 0.0, NEG)
    tri_ref[...] = jnp.where(row > col, 1.0, 0.0).astype(BF16)
    masks = _head_masks((tq, LANES))
    for e in range(2):
        carry_refs[e][...] = jnp.zeros(carry_refs[e].shape, F32)
        acc_refs[e][...] = jnp.zeros(acc_refs[e].shape, F32)

    def logits(n, slot):
        i, kb, diag = steps[n]
        rq = pl.ds(i * tq, tq)
        q = q_ref[0, rq, :]
        k = k_ref[0, pl.ds(kb * tq, tq), :]
        for e in range(2):
            qm = jnp.where(masks[e], q, jnp.zeros_like(q))
            z = _dot_nt(qm, k)
            if diag:
                z = z + mask_ref[...]
            sp = jnp.maximum(z, 0.0) + jnp.log2(1.0 + jnp.exp2(-jnp.abs(z)))
            zs_refs[slot][e][...] = z - sp
            sp_refs[slot][e][...] = sp.astype(BF16)
            carry = carry_refs[e][rq, :]
            c_refs[slot][e][...] = carry
            carry_refs[e][rq, :] = carry + jnp.sum(sp, axis=1, keepdims=True)

    def weights(n, slot):
        for e in range(2):
            right = _dot(sp_refs[slot][e][...], tri_ref[...])
            right = right + jnp.concatenate([c_refs[slot][e][...]] * (tq // LANES), axis=1)
            w_refs[slot][e][...] = jnp.exp2(zs_refs[slot][e][...] - right).astype(BF16)

    def pv(n, slot):
        i, kb, _ = steps[n]
        rq = pl.ds(i * tq, tq)
        v = v_ref[0, pl.ds(kb * tq, tq), :]
        for e in range(2):
            acc_refs[e][rq, :] = acc_refs[e][rq, :] + _dot(w_refs[slot][e][...], v)

    _run_pipeline(len(steps), logits, weights, pv)
    full = _head_masks(acc_refs[0].shape)
    o_ref[0] = jnp.where(full[0], acc_refs[0][...], acc_refs[1][...]).astype(BF16)


def _sb_attention(qkv, *, tq=256):
    batch, seq, _ = qkv.shape
    npair = W_SB // LANES
    steps = _tile_schedule(seq // tq, diag_first=True)
    blk = lambda off: pl.BlockSpec((1, seq, LANES), lambda b, p: (b, 0, off + p))
    zs_shape, sp_shape = pltpu.VMEM((tq, tq), F32), pltpu.VMEM((tq, tq), BF16)
    c_shape, w_shape = pltpu.VMEM((tq, LANES), F32), pltpu.VMEM((tq, tq), BF16)
    acc_shape = pltpu.VMEM((seq, LANES), F32)
    return pl.pallas_call(
        functools.partial(_sb_kernel, tq=tq, steps=steps),
        grid=(batch, npair),
        in_specs=[blk(0), blk(npair), blk(2 * npair)],
        out_specs=pl.BlockSpec((1, seq, LANES), lambda b, p: (b, 0, p)),
        out_shape=jax.ShapeDtypeStruct((batch, seq, W_SB), BF16),
        scratch_shapes=[pltpu.VMEM((tq, tq), F32), pltpu.VMEM((tq, tq), BF16)]
                       + [zs_shape] * 4 + [sp_shape] * 4 + [c_shape] * 4 + [w_shape] * 4 + [acc_shape] * 4,
        compiler_params=_params(("parallel", "parallel")),
        name="sb_attn",
    )(qkv, qkv, qkv)


def _fox_kernel(q_ref, k_ref, v_ref, f_ref, o_ref, mask_ref,
                s00, s01, s10, s11, p00, p01, p10, p11, a00, a01, a10, a11,
                m0, m1, acc0, acc1, va0, va1, *, tq, steps):
    s_refs, p_refs, a_refs = ((s00, s01), (s10, s11)), ((p00, p01), (p10, p11)), ((a00, a01), (a10, a11))
    m_refs, acc_refs, va_refs = (m0, m1), (acc0, acc1), (va0, va1)
    row = lax.broadcasted_iota(jnp.int32, (tq, tq), 0)
    col = lax.broadcasted_iota(jnp.int32, (tq, tq), 1)
    mask_ref[...] = jnp.where(col <= row, 0.0, NEG)
    masks = _head_masks((tq, LANES))
    v_all = v_ref[0]
    full = _head_masks(v_all.shape)
    for e in range(2):
        m_refs[e][...] = jnp.full(m_refs[e].shape, NEG, F32)
        acc_refs[e][...] = jnp.zeros(acc_refs[e].shape, F32)
        va_refs[e][...] = jnp.where(full[e], v_all, jnp.ones_like(v_all))

    def qk(n, slot):
        i, kb, _ = steps[n]
        q = q_ref[0, pl.ds(i * tq, tq), :]
        k = k_ref[0, pl.ds(kb * tq, tq), :]
        for e in range(2):
            qm = jnp.where(masks[e], q, jnp.zeros_like(q))
            s_refs[slot][e][...] = _dot_nt(qm, k)

    def sm(n, slot):
        i, kb, diag = steps[n]
        rq, rk = pl.ds(i * tq, tq), pl.ds(kb * tq, tq)
        for e in range(2):
            fq0 = f_ref[e, :, rq][:, 0:1]
            s = s_refs[slot][e][...] + (fq0 - f_ref[e, :, rk])
            if diag:
                s = s + mask_ref[...]
            m_prev = m_refs[e][rq, :]
            m_new = jnp.maximum(m_prev, jnp.max(s, axis=1, keepdims=True))
            m_refs[e][rq, :] = m_new
            a_refs[slot][e][...] = jnp.exp2(m_prev - m_new)
            p_refs[slot][e][...] = jnp.exp2(s - jnp.concatenate([m_new] * (tq // LANES), axis=1)).astype(BF16)

    def pv(n, slot):
        i, kb, _ = steps[n]
        rq, rk = pl.ds(i * tq, tq), pl.ds(kb * tq, tq)
        for e in range(2):
            acc_refs[e][rq, :] = (a_refs[slot][e][...] * acc_refs[e][rq, :]
                                  + _dot(p_refs[slot][e][...], va_refs[e][rk, :]))

    _run_pipeline(len(steps), qk, sm, pv)
    outs = []
    for e in range(2):
        acc = acc_refs[e][...]
        outs.append(acc * (1.0 / pltpu.roll(acc, HEAD_DIM, axis=1)))
    o_ref[0] = jnp.where(full[0], outs[0], outs[1]).astype(BF16)


def _fox_attention(qkv, fcum, *, tq=256):
    batch, seq, _ = qkv.shape
    cb = (3 * W_SB + 3 * W_CH) // LANES
    npair = W_FOX // LANES
    steps = _tile_schedule(seq // tq, diag_first=False)
    blk = lambda off: pl.BlockSpec((1, seq, LANES), lambda b, p: (b, 0, cb + off + p))
    s_shape, p_shape, a_shape = pltpu.VMEM((tq, tq), F32), pltpu.VMEM((tq, tq), BF16), pltpu.VMEM((tq, LANES), F32)
    acc_shape, v_shape = pltpu.VMEM((seq, LANES), F32), pltpu.VMEM((seq, LANES), BF16)
    return pl.pallas_call(
        functools.partial(_fox_kernel, tq=tq, steps=steps),
        grid=(batch, npair),
        in_specs=[blk(0), blk(npair), blk(2 * npair),
                  pl.BlockSpec((None, 2, 1, seq), lambda b, p: (b, p, 0, 0))],
        out_specs=pl.BlockSpec((1, seq, LANES), lambda b, p: (b, 0, p)),
        out_shape=jax.ShapeDtypeStruct((batch, seq, W_FOX), BF16),
        scratch_shapes=[pltpu.VMEM((tq, tq), F32)] + [s_shape] * 4 + [p_shape] * 4 + [a_shape] * 4
                       + [acc_shape] * 4 + [v_shape] * 2,
        compiler_params=_params(("parallel", "parallel")),
        name="fox_attn",
    )(qkv, qkv, qkv, fcum)


def _chunk_kernel(q_ref, k_ref, v_ref, r_ref, o_ref, bias_ref, s0_ref, s1_ref, p0_ref, p1_ref, l0_ref, l1_ref,
                  *, seq):
    n_tiles = seq // PAIR_Q
    rows = lax.broadcasted_iota(jnp.int32, (PAIR_Q, PAIR_KEYS), 0)
    cols = lax.broadcasted_iota(jnp.int32, (PAIR_Q, PAIR_KEYS), 1)
    first_key = jnp.where(rows < CHUNK, 0, CHUNK)
    valid = (cols >= first_key) & (cols < first_key + PAIR_KEYS - CHUNK)
    masks = _head_masks((PAIR_Q, LANES))
    for e in range(2):
        gen = jnp.broadcast_to(r_ref[0, e:e + 1, :], (PAIR_Q, REL_VEC))
        toeplitz = pltpu.roll(gen, 0, 1, stride=1, stride_axis=0)
        bias_ref[e] = jnp.where(valid, toeplitz[:, :PAIR_KEYS], NEG)
    s_refs, p_refs, l_refs = (s0_ref, s1_ref), (p0_ref, p1_ref), (l0_ref, l1_ref)

    def window(t):
        nk = min(PAIR_KEYS, PAIR_Q * (t + 1))
        return pl.ds(PAIR_Q * (t + 1) - nk, nk), nk

    def qk(t, slot):
        keys, nk = window(t)
        q = q_ref[0, pl.ds(t * PAIR_Q, PAIR_Q), :]
        k = k_ref[0, keys, :]
        for e in range(2):
            qm = jnp.where(masks[e], q, jnp.zeros_like(q))
            s_refs[slot][e, :, :nk] = _dot_nt(qm, k)

    def sm(t, slot):
        _, nk = window(t)
        for e in range(2):
            s = s_refs[slot][e, :, :nk] + bias_ref[e, :, PAIR_KEYS - nk:]
            p = jnp.exp2(s - jnp.max(s, axis=1, keepdims=True))
            l_refs[slot][e] = jnp.sum(p, axis=1, keepdims=True)
            p_refs[slot][e, :, :nk] = p.astype(BF16)

    def pv(t, slot):
        keys, nk = window(t)
        v = v_ref[0, keys, :]
        outs = [_dot(p_refs[slot][e, :, :nk], v) * (1.0 / l_refs[slot][e]) for e in range(2)]
        o_ref[0, pl.ds(t * PAIR_Q, PAIR_Q), :] = jnp.where(masks[0], outs[0], outs[1]).astype(BF16)

    _run_pipeline(n_tiles, qk, sm, pv)


def _chunk_attention(qkv, rel_gen):
    batch, seq, _ = qkv.shape
    cb = 3 * W_SB // LANES
    npair = W_CH // LANES
    blk = lambda off: pl.BlockSpec((1, seq, LANES), lambda b, p: (b, 0, cb + off + p))
    s_shape = pltpu.VMEM((2, PAIR_Q, PAIR_KEYS), F32)
    p_shape = pltpu.VMEM((2, PAIR_Q, PAIR_KEYS), BF16)
    l_shape = pltpu.VMEM((2, PAIR_Q, 1), F32)
    return pl.pallas_call(
        functools.partial(_chunk_kernel, seq=seq),
        grid=(batch, npair),
        in_specs=[blk(0), blk(npair), blk(2 * npair),
                  pl.BlockSpec((1, 2, REL_VEC), lambda b, p: (p, 0, 0))],
        out_specs=pl.BlockSpec((1, seq, LANES), lambda b, p: (b, 0, p)),
        out_shape=jax.ShapeDtypeStruct((batch, seq, W_CH), BF16),
        scratch_shapes=[pltpu.VMEM((2, PAIR_Q, PAIR_KEYS), F32),
                        s_shape, s_shape, p_shape, p_shape, l_shape, l_shape],
        compiler_params=_params(("parallel", "parallel")),
        name="chunk_attn",
    )(qkv, qkv, qkv, rel_gen)


def _merge_kernel(x_ref, oa_ref, ob_ref, oc_ref, g_ref, wg_ref, bg_ref, wa_ref, wb_ref, wc_ref, wo_ref, o_ref):
    x = x_ref[...]
    h = _rms_bf16(x, g_ref[...])
    gates = jax.nn.sigmoid(_dot(h, wg_ref[...]) + bg_ref[...])
    merged = (gates[:, :D_MODEL] * _dot(oa_ref[...], wa_ref[...])
              + gates[:, D_MODEL:2 * D_MODEL] * _dot(ob_ref[...], wb_ref[...])
              + gates[:, 2 * D_MODEL:] * _dot(oc_ref[...], wc_ref[...]))
    o_ref[...] = x + _dot(merged.astype(BF16), wo_ref[...])


def _merge(x, oa, ob, oc, g, wg, bg, wa, wb, wc, wo, *, layer, tm=512):
    n, d = x.shape
    rows = lambda w: pl.BlockSpec((tm, w), lambda i: (i, 0))
    consts = [g, wg, bg, wa, wb, wc, wo]
    return pl.pallas_call(
        _merge_kernel,
        grid=(n // tm,),
        in_specs=[rows(d), rows(oa.shape[1]), rows(ob.shape[1]), rows(oc.shape[1])]
                 + [_layer_spec(c, layer) if c.ndim == 3 else _const_spec(c.shape) for c in consts],
        out_specs=rows(d),
        out_shape=jax.ShapeDtypeStruct((n, d), F32),
        compiler_params=_params(("parallel",)),
        name="merge",
    )(x, oa, ob, oc, *consts)


def _rel_generator_index():
    j = np.arange(REL_VEC)
    dist = np.where(j <= PAIR_KEYS, LEFT_CHUNKS * CHUNK - j, MAX_REL)
    return np.clip(dist, -MAX_REL, MAX_REL) + MAX_REL


def _cast_kernel(w_ref, o_ref):
    o_ref[...] = w_ref[...].astype(BF16)


def _cast_bf16(w, rows):
    l, r, c = w.shape
    blk = pl.BlockSpec((1, rows, c), lambda li, ri: (li, ri, 0))
    return pl.pallas_call(
        _cast_kernel, grid=(l, r // rows), in_specs=[blk], out_specs=blk,
        out_shape=jax.ShapeDtypeStruct(w.shape, BF16),
        compiler_params=_params(("parallel", "parallel")), name="cast",
    )(w)


def _split_w_in_kernel(wq_ref, wg_ref, wf_ref, scale_ref, qkv_ref, g_ref, f_ref):
    nf = N_HEADS_FOX
    for l in range(qkv_ref.shape[0]):
        qkv_ref[l] = (wq_ref[:, l, :] * scale_ref[...]).T.astype(BF16)
        g_ref[l] = wg_ref[:, l, :].T.astype(BF16)
        slab = wf_ref[:, l, :].T
        lane = lax.broadcasted_iota(jnp.int32, slab.shape, 1)
        f_ref[l] = jnp.where(lane < nf, slab, 0.0).astype(BF16)


def _split_w_in(w_in_t, col_scale_t, cols=512):
    width, l, d = w_in_t.shape
    ng = width - QKV_WIDTH - N_HEADS_FOX
    src = lambda n, start: pl.BlockSpec((pl.Element(n), pl.Element(l), pl.Element(d)),
                                        lambda j: (start + j * n, 0, 0))
    dst = lambda c: pl.BlockSpec((l, d, c), lambda j: (0, 0, j))
    return pl.pallas_call(
        _split_w_in_kernel, grid=(QKV_WIDTH // cols,),
        in_specs=[src(cols, 0), src(cols, QKV_WIDTH + N_HEADS_FOX),
                  pl.BlockSpec((pl.Element(LANES), pl.Element(l), pl.Element(d)), lambda j: (QKV_WIDTH, 0, 0)),
                  pl.BlockSpec((cols, 1), lambda j: (j, 0))],
        out_specs=[dst(cols), dst(cols), pl.BlockSpec((l, d, LANES), lambda j: (0, 0, 0))],
        out_shape=[jax.ShapeDtypeStruct((l, d, QKV_WIDTH), BF16), jax.ShapeDtypeStruct((l, d, ng), BF16),
                   jax.ShapeDtypeStruct((l, d, LANES), BF16)],
        compiler_params=_params(("arbitrary",)), name="split_w_in",
    )(w_in_t, w_in_t, w_in_t, col_scale_t)


def _q_col_scale(width):
    col_scale = np.ones((1, width), np.float32)
    for start, w in ((0, W_SB), (3 * W_SB, W_CH), (3 * W_SB + 3 * W_CH, W_FOX)):
        col_scale[:, start:start + w] = HEAD_DIM ** -0.5 * LOG2E
    return jnp.asarray(col_scale)


def kernel(x, g_ffn1, w_ffn1_in, w_ffn1_out, g_mix, w_in, b_in, rel_bias, w_br_sb, w_br_ch, w_br_fox, w_out,
           g_ffn2, w_ffn2_in, w_ffn2_out, g_final):
    batch, seq, d = x.shape
    depth = g_ffn1.shape[0]
    nf = N_HEADS_FOX
    col_scale = _q_col_scale(w_in.shape[-1])
    ffn1_in, ffn2_in = _cast_bf16(w_ffn1_in, 256), _cast_bf16(w_ffn2_in, 256)
    ffn1_out, ffn2_out = _cast_bf16(w_ffn1_out, 704), _cast_bf16(w_ffn2_out, 704)
    br_sb, br_ch, br_fox = _cast_bf16(w_br_sb, W_SB), _cast_bf16(w_br_ch, W_CH), _cast_bf16(w_br_fox, W_FOX)
    wo = _cast_bf16(w_out, d)
    w_qkv, w_g, w_f = _split_w_in(jnp.transpose(w_in, (2, 0, 1)), col_scale[0, :QKV_WIDTH, None])
    b_s = b_in * col_scale
    b_qkv = b_s[:, :QKV_WIDTH]
    b_f = jnp.pad(b_s[:, QKV_WIDTH:QKV_WIDTH + nf], ((0, 0), (0, LANES - nf)))
    b_g = b_s[:, QKV_WIDTH + nf:]
    rel_gen = (rel_bias[:, _rel_generator_index(), :] * LOG2E).transpose(0, 2, 1).reshape(
        depth, N_HEADS_CH // 2, 2, REL_VEC)
    xf = x.reshape(batch * seq, d)
    gf = g_final[None, :]
    for l in range(depth):
        xf = _ffn(xf, g_ffn1[l][None, :], ffn1_in, ffn1_out, gf, layer=l, final_norm=False)
        qkv, fcum = _inproj(xf, g_mix[l][None, :], w_qkv, b_qkv[l][None, :], w_f, b_f[l][None, :],
                            layer=l, batch=batch)
        qkv = qkv.reshape(batch, seq, QKV_WIDTH)
        o_a = _sb_attention(qkv)
        o_b = _chunk_attention(qkv, rel_gen[l])
        o_c = _fox_attention(qkv, fcum[:, :nf].reshape(batch, nf, 1, seq))
        xf = _merge(xf, o_a.reshape(batch * seq, W_SB), o_b.reshape(batch * seq, W_CH),
                    o_c.reshape(batch * seq, W_FOX), g_mix[l][None, :], w_g, b_g[l][None, :],
                    br_sb, br_ch, br_fox, wo, layer=l)
        xf = _ffn(xf, g_ffn2[l][None, :], ffn2_in, ffn2_out, gf, layer=l, final_norm=(l == depth - 1))
    return xf.reshape(batch, seq, d)
```

```python
import functools

import numpy as np
import jax
import jax.numpy as jnp
from jax import lax
from jax.experimental import pallas as pl
from jax.experimental.pallas import tpu as pltpu

D_MODEL = 1024
HEAD_DIM = 64
CHUNK = 64
LEFT_CHUNKS = 8
MAX_REL = 128
N_HEADS_SB = 4
N_HEADS_CH = 8
N_HEADS_FOX = 4
W_SB = N_HEADS_SB * HEAD_DIM
W_CH = N_HEADS_CH * HEAD_DIM
W_FOX = N_HEADS_FOX * HEAD_DIM
QKV_WIDTH = 3 * (W_SB + W_CH + W_FOX)
RMS_EPS = 1e-6
NEG = -1e30
LOG2E = 1.4426950408889634

LANES = 128
PAIR_Q = 2 * CHUNK
PAIR_KEYS = PAIR_Q + LEFT_CHUNKS * CHUNK
REL_VEC = 768
VMEM_LIMIT = 56 * 1024 * 1024

F32 = jnp.float32
BF16 = jnp.bfloat16


def _rms_bf16(x, g):
    ms = jnp.mean(x * x, axis=-1, keepdims=True)
    return ((x * lax.rsqrt(ms + RMS_EPS)) * g).astype(BF16)


def _dot(a, b):
    return jnp.dot(a, b, preferred_element_type=F32)


def _dot_nt(a, b):
    return lax.dot_general(a, b, (((1,), (1,)), ((), ())), preferred_element_type=F32)


def _const_spec(shape):
    nd = len(shape)
    return pl.BlockSpec(shape, lambda *_: (0,) * nd, pipeline_mode=pl.Buffered(1))


def _layer_spec(w, layer, col_blocks=1, col=0):
    _, r, c = w.shape
    return pl.BlockSpec((None, r, c // col_blocks), lambda *_: (layer, 0, col), pipeline_mode=pl.Buffered(1))


def _params(sem):
    return pltpu.CompilerParams(dimension_semantics=sem, vmem_limit_bytes=VMEM_LIMIT)


def _ffn_kernel(x_ref, g_ref, wg_ref, wu_ref, wo_ref, gf_ref, o_ref, *, final_norm):
    x = x_ref[...]
    h = _rms_bf16(x, g_ref[...])
    gate = _dot(h, wg_ref[...])
    up = _dot(h, wu_ref[...])
    a = (gate * jax.nn.sigmoid(gate) * up).astype(BF16)
    y = x + 0.5 * _dot(a, wo_ref[...])
    if final_norm:
        ms = jnp.mean(y * y, axis=-1, keepdims=True)
        y = (y * lax.rsqrt(ms + RMS_EPS)) * gf_ref[...]
    o_ref[...] = y


def _ffn(x, g, w_in, w_out, gf, *, layer, final_norm, tm=512):
    n, d = x.shape
    row = pl.BlockSpec((tm, d), lambda i: (i, 0))
    return pl.pallas_call(
        functools.partial(_ffn_kernel, final_norm=final_norm),
        grid=(n // tm,),
        in_specs=[row, _const_spec(g.shape), _layer_spec(w_in, layer, 2, 0), _layer_spec(w_in, layer, 2, 1),
                  _layer_spec(w_out, layer), _const_spec(gf.shape)],
        out_specs=row,
        out_shape=jax.ShapeDtypeStruct((n, d), F32),
        compiler_params=_params(("parallel",)),
        name="ffn",
    )(x, g, w_in, w_in, w_out, gf)


def _inproj_kernel(x_ref, g_ref, w_ref, b_ref, wf_ref, bf_ref, qkv_ref, fc_ref, carry_ref, *, tm):
    @pl.when(pl.program_id(1) == 0)
    def _():
        carry_ref[...] = jnp.zeros_like(carry_ref)

    h = _rms_bf16(x_ref[...], g_ref[...])
    f = _dot(h, wf_ref[...]) + bf_ref[...]
    lf = (jnp.minimum(f, 0.0) - jnp.log1p(jnp.exp(-jnp.abs(f)))) * LOG2E
    s = lf.T[:8, :]
    lane = lax.broadcasted_iota(jnp.int32, s.shape, 1)
    sh = 1
    while sh < tm:
        s = s + jnp.where(lane >= sh, pltpu.roll(s, sh, axis=1), 0.0)
        sh *= 2
    fc = s + carry_ref[:, 0:1]
    fc_ref[0] = fc
    carry_ref[...] = jnp.broadcast_to(fc[:, tm - 1:tm], carry_ref.shape)
    qkv_ref[...] = (_dot(h, w_ref[...]) + b_ref[...]).astype(BF16)


def _inproj(x, g, w, b, wf, bfg, *, layer, batch, tm=512):
    n, d = x.shape
    seq = n // batch
    nt = seq // tm
    return pl.pallas_call(
        functools.partial(_inproj_kernel, tm=tm),
        grid=(batch, nt),
        in_specs=[pl.BlockSpec((tm, d), lambda bi, ti: (bi * nt + ti, 0)),
                  _const_spec(g.shape), _layer_spec(w, layer), _const_spec(b.shape),
                  _layer_spec(wf, layer), _const_spec(bfg.shape)],
        out_specs=[pl.BlockSpec((tm, QKV_WIDTH), lambda bi, ti: (bi * nt + ti, 0)),
                   pl.BlockSpec((1, 8, tm), lambda bi, ti: (bi, 0, ti))],
        out_shape=[jax.ShapeDtypeStruct((n, QKV_WIDTH), BF16),
                   jax.ShapeDtypeStruct((batch, 8, seq), F32)],
        scratch_shapes=[pltpu.VMEM((8, LANES), F32)],
        compiler_params=_params(("parallel", "arbitrary")),
        name="inproj",
    )(x, g, w, b, wf, bfg)


def _head_masks(shape):
    lane = lax.broadcasted_iota(jnp.int32, shape, 1)
    lo = lane < HEAD_DIM
    return lo, jnp.logical_not(lo)


def _tile_schedule(n_tiles, diag_first):
    steps = []
    for i in range(n_tiles):
        kbs = range(i, -1, -1) if diag_first else range(i + 1)
        steps += [(i, kb, kb == i) for kb in kbs]
    return steps


def _run_pipeline(n_steps, stage0, stage1, stage2, order=(0, 1, 2)):
    for t in range(n_steps + 2):
        for which in order:
            if which == 0 and t < n_steps:
                stage0(t, t % 2)
            if which == 1 and 1 <= t <= n_steps:
                stage1(t - 1, (t - 1) % 2)
            if which == 2 and t >= 2:
                stage2(t - 2, t % 2)


def _sb_kernel(q_ref, k_ref, v_ref, o_ref, mask_ref, tri_ref, zs0, zs1, sp0, sp1, c0, c1, w0, w1,
               carry_ref, acc_ref, *, tq, steps):
    zs_refs, sp_refs, c_refs, w_refs = (zs0, zs1), (sp0, sp1), (c0, c1), (w0, w1)
    row = lax.broadcasted_iota(jnp.int32, (tq, tq), 0)
    col = lax.broadcasted_iota(jnp.int32, (tq, tq), 1)
    causal = jnp.where(col < row, 0.0, NEG)
    mask_ref[:tq, :] = causal
    mask_ref[tq:, :] = causal
    tri_ref[...] = jnp.where(row > col, 1.0, 0.0).astype(BF16)
    masks = _head_masks((tq, LANES))
    carry_ref[...] = jnp.zeros(carry_ref.shape, F32)
    acc_ref[...] = jnp.zeros(acc_ref.shape, F32)

    def logits(n, slot):
        i, kb, diag = steps[n]
        q = q_ref[0, pl.ds(i * tq, tq), :]
        q2 = jnp.concatenate([jnp.where(masks[e], q, jnp.zeros_like(q)) for e in range(2)], axis=0)
        z = _dot_nt(q2, k_ref[0, pl.ds(kb * tq, tq), :])
        if diag:
            z = z + mask_ref[...]
        sp = jnp.maximum(z, 0.0) + jnp.log2(1.0 + jnp.exp2(-jnp.abs(z)))
        zs_refs[slot][...] = z - sp
        sp_refs[slot][...] = sp.astype(BF16)
        carry = carry_ref[i]
        c_refs[slot][...] = carry
        carry_ref[i] = carry + jnp.sum(sp, axis=1, keepdims=True)

    def weights(n, slot):
        right = _dot(sp_refs[slot][...], tri_ref[...])
        right = right + jnp.concatenate([c_refs[slot][...]] * (tq // LANES), axis=1)
        w_refs[slot][...] = jnp.exp2(zs_refs[slot][...] - right).astype(BF16)

    def pv(n, slot):
        i, kb, _ = steps[n]
        acc_ref[i] = acc_ref[i] + _dot(w_refs[slot][...], v_ref[0, pl.ds(kb * tq, tq), :])

    _run_pipeline(len(steps), logits, weights, pv, order=(2, 0, 1))
    for i in range(acc_ref.shape[0]):
        acc = acc_ref[i]
        o_ref[0, pl.ds(i * tq, tq), :] = jnp.where(masks[0], acc[:tq], acc[tq:]).astype(BF16)


def _sb_attention(qkv, *, tq=256):
    batch, seq, _ = qkv.shape
    npair = W_SB // LANES
    steps = _tile_schedule(seq // tq, diag_first=True)
    blk = lambda off: pl.BlockSpec((1, seq, LANES), lambda b, p: (b, 0, off + p))
    zs_shape, sp_shape = pltpu.VMEM((2 * tq, tq), F32), pltpu.VMEM((2 * tq, tq), BF16)
    c_shape, w_shape = pltpu.VMEM((2 * tq, LANES), F32), pltpu.VMEM((2 * tq, tq), BF16)
    state_shape = pltpu.VMEM((seq // tq, 2 * tq, LANES), F32)
    return pl.pallas_call(
        functools.partial(_sb_kernel, tq=tq, steps=steps),
        grid=(batch, npair),
        in_specs=[blk(0), blk(npair), blk(2 * npair)],
        out_specs=pl.BlockSpec((1, seq, LANES), lambda b, p: (b, 0, p)),
        out_shape=jax.ShapeDtypeStruct((batch, seq, W_SB), BF16),
        scratch_shapes=[pltpu.VMEM((2 * tq, tq), F32), pltpu.VMEM((tq, tq), BF16)]
                       + [zs_shape] * 2 + [sp_shape] * 2 + [c_shape] * 2 + [w_shape] * 2 + [state_shape] * 2,
        compiler_params=_params(("parallel", "parallel")),
        name="sb_attn",
    )(qkv, qkv, qkv)


def _fox_kernel(q_ref, k_ref, v_ref, f_ref, o_ref, mask_ref,
                s00, s01, s10, s11, p00, p01, p10, p11, a00, a01, a10, a11,
                m0, m1, acc0, acc1, va0, va1, *, tq, steps):
    s_refs, p_refs, a_refs = ((s00, s01), (s10, s11)), ((p00, p01), (p10, p11)), ((a00, a01), (a10, a11))
    m_refs, acc_refs, va_refs = (m0, m1), (acc0, acc1), (va0, va1)
    row = lax.broadcasted_iota(jnp.int32, (tq, tq), 0)
    col = lax.broadcasted_iota(jnp.int32, (tq, tq), 1)
    mask_ref[...] = jnp.where(col <= row, 0.0, NEG)
    masks = _head_masks((tq, LANES))
    v_all = v_ref[0]
    full = _head_masks(v_all.shape)
    for e in range(2):
        m_refs[e][...] = jnp.full(m_refs[e].shape, NEG, F32)
        acc_refs[e][...] = jnp.zeros(acc_refs[e].shape, F32)
        va_refs[e][...] = jnp.where(full[e], v_all, jnp.ones_like(v_all))

    def qk(n, slot):
        i, kb, _ = steps[n]
        q = q_ref[0, pl.ds(i * tq, tq), :]
        k = k_ref[0, pl.ds(kb * tq, tq), :]
        for e in range(2):
            qm = jnp.where(masks[e], q, jnp.zeros_like(q))
            s_refs[slot][e][...] = _dot_nt(qm, k)

    def sm(n, slot):
        i, kb, diag = steps[n]
        rq, rk = pl.ds(i * tq, tq), pl.ds(kb * tq, tq)
        for e in range(2):
            fq0 = f_ref[e, :, rq][:, 0:1]
            s = s_refs[slot][e][...] + (fq0 - f_ref[e, :, rk])
            if diag:
                s = s + mask_ref[...]
            m_prev = m_refs[e][rq, :]
            m_new = jnp.maximum(m_prev, jnp.max(s, axis=1, keepdims=True))
            m_refs[e][rq, :] = m_new
            a_refs[slot][e][...] = jnp.exp2(m_prev - m_new)
            p_refs[slot][e][...] = jnp.exp2(s - jnp.concatenate([m_new] * (tq // LANES), axis=1)).astype(BF16)

    def pv(n, slot):
        i, kb, _ = steps[n]
        rq, rk = pl.ds(i * tq, tq), pl.ds(kb * tq, tq)
        for e in range(2):
            acc_refs[e][rq, :] = (a_refs[slot][e][...] * acc_refs[e][rq, :]
                                  + _dot(p_refs[slot][e][...], va_refs[e][rk, :]))

    _run_pipeline(len(steps), qk, sm, pv, order=(2, 0, 1))
    outs = []
    for e in range(2):
        acc = acc_refs[e][...]
        outs.append(acc * (1.0 / pltpu.roll(acc, HEAD_DIM, axis=1)))
    o_ref[0] = jnp.where(full[0], outs[0], outs[1]).astype(BF16)


def _fox_attention(qkv, fcum, *, tq=256):
    batch, seq, _ = qkv.shape
    cb = (3 * W_SB + 3 * W_CH) // LANES
    npair = W_FOX // LANES
    steps = _tile_schedule(seq // tq, diag_first=False)
    blk = lambda off: pl.BlockSpec((1, seq, LANES), lambda b, p: (b, 0, cb + off + p))
    s_shape, p_shape, a_shape = pltpu.VMEM((tq, tq), F32), pltpu.VMEM((tq, tq), BF16), pltpu.VMEM((tq, LANES), F32)
    acc_shape, v_shape = pltpu.VMEM((seq, LANES), F32), pltpu.VMEM((seq, LANES), BF16)
    return pl.pallas_call(
        functools.partial(_fox_kernel, tq=tq, steps=steps),
        grid=(batch, npair),
        in_specs=[blk(0), blk(npair), blk(2 * npair),
                  pl.BlockSpec((None, 2, 1, seq), lambda b, p: (b, p, 0, 0))],
        out_specs=pl.BlockSpec((1, seq, LANES), lambda b, p: (b, 0, p)),
        out_shape=jax.ShapeDtypeStruct((batch, seq, W_FOX), BF16),
        scratch_shapes=[pltpu.VMEM((tq, tq), F32)] + [s_shape] * 4 + [p_shape] * 4 + [a_shape] * 4
                       + [acc_shape] * 4 + [v_shape] * 2,
        compiler_params=_params(("parallel", "parallel")),
        name="fox_attn",
    )(qkv, qkv, qkv, fcum)


def _chunk_kernel(q_ref, k_ref, v_ref, r_ref, o_ref, bias_ref, s0_ref, s1_ref, p0_ref, p1_ref, l0_ref, l1_ref,
                  *, seq):
    n_tiles = seq // PAIR_Q
    rows = lax.broadcasted_iota(jnp.int32, (PAIR_Q, PAIR_KEYS), 0)
    cols = lax.broadcasted_iota(jnp.int32, (PAIR_Q, PAIR_KEYS), 1)
    first_key = jnp.where(rows < CHUNK, 0, CHUNK)
    valid = (cols >= first_key) & (cols < first_key + PAIR_KEYS - CHUNK)
    masks = _head_masks((PAIR_Q, LANES))
    for e in range(2):
        gen = jnp.broadcast_to(r_ref[0, e:e + 1, :], (PAIR_Q, REL_VEC))
        toeplitz = pltpu.roll(gen, 0, 1, stride=1, stride_axis=0)
        bias_ref[e * PAIR_Q:(e + 1) * PAIR_Q, :] = jnp.where(valid, toeplitz[:, :PAIR_KEYS], NEG)
    s_refs, p_refs, l_refs = (s0_ref, s1_ref), (p0_ref, p1_ref), (l0_ref, l1_ref)

    def window(t):
        nk = min(PAIR_KEYS, PAIR_Q * (t + 1))
        return pl.ds(PAIR_Q * (t + 1) - nk, nk), nk

    def qk(t, slot):
        keys, nk = window(t)
        q = q_ref[0, pl.ds(t * PAIR_Q, PAIR_Q), :]
        q2 = jnp.concatenate([jnp.where(masks[e], q, jnp.zeros_like(q)) for e in range(2)], axis=0)
        s_refs[slot][:, :nk] = _dot_nt(q2, k_ref[0, keys, :])

    def sm(t, slot):
        _, nk = window(t)
        s = s_refs[slot][:, :nk] + bias_ref[:, PAIR_KEYS - nk:]
        p = jnp.exp2(s - jnp.max(s, axis=1, keepdims=True))
        l_refs[slot][...] = jnp.sum(p, axis=1, keepdims=True)
        p_refs[slot][:, :nk] = p.astype(BF16)

    def pv(t, slot):
        keys, nk = window(t)
        o2 = _dot(p_refs[slot][:, :nk], v_ref[0, keys, :]) * (1.0 / l_refs[slot][...])
        o_ref[0, pl.ds(t * PAIR_Q, PAIR_Q), :] = jnp.where(masks[0], o2[:PAIR_Q], o2[PAIR_Q:]).astype(BF16)

    _run_pipeline(n_tiles, qk, sm, pv)


def _chunk_attention(qkv, rel_gen):
    batch, seq, _ = qkv.shape
    cb = 3 * W_SB // LANES
    npair = W_CH // LANES
    blk = lambda off: pl.BlockSpec((1, seq, LANES), lambda b, p: (b, 0, cb + off + p))
    s_shape = pltpu.VMEM((2 * PAIR_Q, PAIR_KEYS), F32)
    p_shape = pltpu.VMEM((2 * PAIR_Q, PAIR_KEYS), BF16)
    l_shape = pltpu.VMEM((2 * PAIR_Q, 1), F32)
    return pl.pallas_call(
        functools.partial(_chunk_kernel, seq=seq),
        grid=(batch, npair),
        in_specs=[blk(0), blk(npair), blk(2 * npair),
                  pl.BlockSpec((1, 2, REL_VEC), lambda b, p: (p, 0, 0))],
        out_specs=pl.BlockSpec((1, seq, LANES), lambda b, p: (b, 0, p)),
        out_shape=jax.ShapeDtypeStruct((batch, seq, W_CH), BF16),
        scratch_shapes=[s_shape, s_shape, s_shape, p_shape, p_shape, l_shape, l_shape],
        compiler_params=_params(("parallel", "parallel")),
        name="chunk_attn",
    )(qkv, qkv, qkv, rel_gen)


def _merge_kernel(x_ref, oa_ref, ob_ref, oc_ref, g_ref, wg_ref, bg_ref, wa_ref, wb_ref, wc_ref, wo_ref, o_ref):
    x = x_ref[...]
    h = _rms_bf16(x, g_ref[...])
    gates = jax.nn.sigmoid(_dot(h, wg_ref[...]) + bg_ref[...])
    merged = (gates[:, :D_MODEL] * _dot(oa_ref[...], wa_ref[...])
              + gates[:, D_MODEL:2 * D_MODEL] * _dot(ob_ref[...], wb_ref[...])
              + gates[:, 2 * D_MODEL:] * _dot(oc_ref[...], wc_ref[...]))
    o_ref[...] = x + _dot(merged.astype(BF16), wo_ref[...])


def _merge(x, oa, ob, oc, g, wg, bg, wa, wb, wc, wo, *, layer, tm=512):
    n, d = x.shape
    rows = lambda w: pl.BlockSpec((tm, w), lambda i: (i, 0))
    consts = [g, wg, bg, wa, wb, wc, wo]
    return pl.pallas_call(
        _merge_kernel,
        grid=(n // tm,),
        in_specs=[rows(d), rows(oa.shape[1]), rows(ob.shape[1]), rows(oc.shape[1])]
                 + [_layer_spec(c, layer) if c.ndim == 3 else _const_spec(c.shape) for c in consts],
        out_specs=rows(d),
        out_shape=jax.ShapeDtypeStruct((n, d), F32),
        compiler_params=_params(("parallel",)),
        name="merge",
    )(x, oa, ob, oc, *consts)


def _rel_generator_index():
    j = np.arange(REL_VEC)
    dist = np.where(j <= PAIR_KEYS, LEFT_CHUNKS * CHUNK - j, MAX_REL)
    return np.clip(dist, -MAX_REL, MAX_REL) + MAX_REL


def _cast_kernel(w_ref, o_ref):
    o_ref[...] = w_ref[...].astype(BF16)


def _cast_bf16(w, rows):
    l, r, c = w.shape
    blk = pl.BlockSpec((1, rows, c), lambda li, ri: (li, ri, 0))
    return pl.pallas_call(
        _cast_kernel, grid=(l, r // rows), in_specs=[blk], out_specs=blk,
        out_shape=jax.ShapeDtypeStruct(w.shape, BF16),
        compiler_params=_params(("parallel", "parallel")), name="cast",
    )(w)


def _split_w_in_kernel(wq_ref, wg_ref, wf_ref, scale_ref, qkv_ref, g_ref, f_ref):
    nf = N_HEADS_FOX
    for l in range(qkv_ref.shape[0]):
        qkv_ref[l] = (wq_ref[:, l, :] * scale_ref[...]).T.astype(BF16)
        g_ref[l] = wg_ref[:, l, :].T.astype(BF16)
        slab = wf_ref[:, l, :].T
        lane = lax.broadcasted_iota(jnp.int32, slab.shape, 1)
        f_ref[l] = jnp.where(lane < nf, slab, 0.0).astype(BF16)


def _split_w_in(w_in_t, col_scale_t, cols=512):
    width, l, d = w_in_t.shape
    ng = width - QKV_WIDTH - N_HEADS_FOX
    src = lambda n, start: pl.BlockSpec((pl.Element(n), pl.Element(l), pl.Element(d)),
                                        lambda j: (start + j * n, 0, 0))
    dst = lambda c: pl.BlockSpec((l, d, c), lambda j: (0, 0, j))
    return pl.pallas_call(
        _split_w_in_kernel, grid=(QKV_WIDTH // cols,),
        in_specs=[src(cols, 0), src(cols, QKV_WIDTH + N_HEADS_FOX),
                  pl.BlockSpec((pl.Element(LANES), pl.Element(l), pl.Element(d)), lambda j: (QKV_WIDTH, 0, 0)),
                  pl.BlockSpec((cols, 1), lambda j: (j, 0))],
        out_specs=[dst(cols), dst(cols), pl.BlockSpec((l, d, LANES), lambda j: (0, 0, 0))],
        out_shape=[jax.ShapeDtypeStruct((l, d, QKV_WIDTH), BF16), jax.ShapeDtypeStruct((l, d, ng), BF16),
                   jax.ShapeDtypeStruct((l, d, LANES), BF16)],
        compiler_params=_params(("arbitrary",)), name="split_w_in",
    )(w_in_t, w_in_t, w_in_t, col_scale_t)


def _q_col_scale(width):
    col_scale = np.ones((1, width), np.float32)
    for start, w in ((0, W_SB), (3 * W_SB, W_CH), (3 * W_SB + 3 * W_CH, W_FOX)):
        col_scale[:, start:start + w] = HEAD_DIM ** -0.5 * LOG2E
    return jnp.asarray(col_scale)


def kernel(x, g_ffn1, w_ffn1_in, w_ffn1_out, g_mix, w_in, b_in, rel_bias, w_br_sb, w_br_ch, w_br_fox, w_out,
           g_ffn2, w_ffn2_in, w_ffn2_out, g_final):
    batch, seq, d = x.shape
    depth = g_ffn1.shape[0]
    nf = N_HEADS_FOX
    col_scale = _q_col_scale(w_in.shape[-1])
    ffn1_in, ffn2_in = _cast_bf16(w_ffn1_in, 256), _cast_bf16(w_ffn2_in, 256)
    ffn1_out, ffn2_out = _cast_bf16(w_ffn1_out, 704), _cast_bf16(w_ffn2_out, 704)
    br_sb, br_ch, br_fox = _cast_bf16(w_br_sb, W_SB), _cast_bf16(w_br_ch, W_CH), _cast_bf16(w_br_fox, W_FOX)
    wo = _cast_bf16(w_out, d)
    w_qkv, w_g, w_f = _split_w_in(jnp.transpose(w_in, (2, 0, 1)), col_scale[0, :QKV_WIDTH, None])
    b_s = b_in * col_scale
    b_qkv = b_s[:, :QKV_WIDTH]
    b_f = jnp.pad(b_s[:, QKV_WIDTH:QKV_WIDTH + nf], ((0, 0), (0, LANES - nf)))
    b_g = b_s[:, QKV_WIDTH + nf:]
    rel_gen = (rel_bias[:, _rel_generator_index(), :] * LOG2E).transpose(0, 2, 1).reshape(
        depth, N_HEADS_CH // 2, 2, REL_VEC)
    xf = x.reshape(batch * seq, d)
    gf = g_final[None, :]
    for l in range(depth):
        xf = _ffn(xf, g_ffn1[l][None, :], ffn1_in, ffn1_out, gf, layer=l, final_norm=False)
        qkv, fcum = _inproj(xf, g_mix[l][None, :], w_qkv, b_qkv[l][None, :], w_f, b_f[l][None, :],
                            layer=l, batch=batch)
        qkv = qkv.reshape(batch, seq, QKV_WIDTH)
        o_a = _sb_attention(qkv)
        o_b = _chunk_attention(qkv, rel_gen[l])
        o_c = _fox_attention(qkv, fcum[:, :nf].reshape(batch, nf, 1, seq))
        xf = _merge(xf, o_a.reshape(batch * seq, W_SB), o_b.reshape(batch * seq, W_CH),
                    o_c.reshape(batch * seq, W_FOX), g_mix[l][None, :], w_g, b_g[l][None, :],
                    br_sb, br_ch, br_fox, wo, layer=l)
        xf = _ffn(xf, g_ffn2[l][None, :], ffn2_in, ffn2_out, gf, layer=l, final_norm=(l == depth - 1))
    return xf.reshape(batch, seq, d)
```

```python
import functools

import numpy as np
import jax
import jax.numpy as jnp
from jax import lax
from jax.experimental import pallas as pl
from jax.experimental.pallas import tpu as pltpu

D_MODEL = 1024
HEAD_DIM = 64
CHUNK = 64
LEFT_CHUNKS = 8
MAX_REL = 128
N_HEADS_SB = 4
N_HEADS_CH = 8
N_HEADS_FOX = 4
W_SB = N_HEADS_SB * HEAD_DIM
W_CH = N_HEADS_CH * HEAD_DIM
W_FOX = N_HEADS_FOX * HEAD_DIM
QKV_WIDTH = 3 * (W_SB + W_CH + W_FOX)
RMS_EPS = 1e-6
NEG = -1e30
LOG2E = 1.4426950408889634

LANES = 128
PAIR_Q = 2 * CHUNK
PAIR_KEYS = PAIR_Q + LEFT_CHUNKS * CHUNK
REL_VEC = 768
VMEM_LIMIT = 56 * 1024 * 1024

F32 = jnp.float32
BF16 = jnp.bfloat16


def _rms_bf16(x, g):
    ms = jnp.mean(x * x, axis=-1, keepdims=True)
    return ((x * lax.rsqrt(ms + RMS_EPS)) * g).astype(BF16)


def _dot(a, b):
    return jnp.dot(a, b, preferred_element_type=F32)


def _dot_nt(a, b):
    return lax.dot_general(a, b, (((1,), (1,)), ((), ())), preferred_element_type=F32)


def _const_spec(shape):
    nd = len(shape)
    return pl.BlockSpec(shape, lambda *_: (0,) * nd, pipeline_mode=pl.Buffered(1))


def _layer_spec(w, layer, col_blocks=1, col=0):
    _, r, c = w.shape
    return pl.BlockSpec((None, r, c // col_blocks), lambda *_: (layer, 0, col), pipeline_mode=pl.Buffered(1))


def _params(sem):
    return pltpu.CompilerParams(dimension_semantics=sem, vmem_limit_bytes=VMEM_LIMIT)


def _ffn_kernel(x_ref, g_ref, wg_ref, wu_ref, wo_ref, gf_ref, o_ref, *, final_norm):
    x = x_ref[...]
    h = _rms_bf16(x, g_ref[...])
    gate = _dot(h, wg_ref[...])
    up = _dot(h, wu_ref[...])
    a = (gate * jax.nn.sigmoid(gate) * up).astype(BF16)
    y = x + 0.5 * _dot(a, wo_ref[...])
    if final_norm:
        ms = jnp.mean(y * y, axis=-1, keepdims=True)
        y = (y * lax.rsqrt(ms + RMS_EPS)) * gf_ref[...]
    o_ref[...] = y


def _ffn(x, g, w_in, w_out, gf, *, layer, final_norm, tm=512):
    n, d = x.shape
    row = pl.BlockSpec((tm, d), lambda i: (i, 0))
    return pl.pallas_call(
        functools.partial(_ffn_kernel, final_norm=final_norm),
        grid=(n // tm,),
        in_specs=[row, _const_spec(g.shape), _layer_spec(w_in, layer, 2, 0), _layer_spec(w_in, layer, 2, 1),
                  _layer_spec(w_out, layer), _const_spec(gf.shape)],
        out_specs=row,
        out_shape=jax.ShapeDtypeStruct((n, d), F32),
        compiler_params=_params(("parallel",)),
        name="ffn",
    )(x, g, w_in, w_in, w_out, gf)


def _inproj_kernel(x_ref, g_ref, w_ref, b_ref, wf_ref, bf_ref, qkv_ref, fc_ref, carry_ref, *, tm):
    @pl.when(pl.program_id(1) == 0)
    def _():
        carry_ref[...] = jnp.zeros_like(carry_ref)

    h = _rms_bf16(x_ref[...], g_ref[...])
    f = _dot(h, wf_ref[...]) + bf_ref[...]
    lf = (jnp.minimum(f, 0.0) - jnp.log1p(jnp.exp(-jnp.abs(f)))) * LOG2E
    s = lf.T[:8, :]
    lane = lax.broadcasted_iota(jnp.int32, s.shape, 1)
    sh = 1
    while sh < tm:
        s = s + jnp.where(lane >= sh, pltpu.roll(s, sh, axis=1), 0.0)
        sh *= 2
    fc = s + carry_ref[:, 0:1]
    fc_ref[0] = fc
    carry_ref[...] = jnp.broadcast_to(fc[:, tm - 1:tm], carry_ref.shape)
    qkv_ref[...] = (_dot(h, w_ref[...]) + b_ref[...]).astype(BF16)


def _inproj(x, g, w, b, wf, bfg, *, layer, batch, tm=512):
    n, d = x.shape
    seq = n // batch
    nt = seq // tm
    return pl.pallas_call(
        functools.partial(_inproj_kernel, tm=tm),
        grid=(batch, nt),
        in_specs=[pl.BlockSpec((tm, d), lambda bi, ti: (bi * nt + ti, 0)),
                  _const_spec(g.shape), _layer_spec(w, layer), _const_spec(b.shape),
                  _layer_spec(wf, layer), _const_spec(bfg.shape)],
        out_specs=[pl.BlockSpec((tm, QKV_WIDTH), lambda bi, ti: (bi * nt + ti, 0)),
                   pl.BlockSpec((1, 8, tm), lambda bi, ti: (bi, 0, ti))],
        out_shape=[jax.ShapeDtypeStruct((n, QKV_WIDTH), BF16),
                   jax.ShapeDtypeStruct((batch, 8, seq), F32)],
        scratch_shapes=[pltpu.VMEM((8, LANES), F32)],
        compiler_params=_params(("parallel", "arbitrary")),
        name="inproj",
    )(x, g, w, b, wf, bfg)


def _head_masks(shape):
    lane = lax.broadcasted_iota(jnp.int32, shape, 1)
    lo = lane < HEAD_DIM
    return lo, jnp.logical_not(lo)


def _tile_schedule(n_tiles, diag_first):
    steps = []
    for i in range(n_tiles):
        kbs = range(i, -1, -1) if diag_first else range(i + 1)
        steps += [(i, kb, kb == i) for kb in kbs]
    return steps


def _run_pipeline(n_steps, stage0, stage1, stage2, order=(0, 1, 2)):
    for t in range(n_steps + 2):
        for which in order:
            if which == 0 and t < n_steps:
                stage0(t, t % 2)
            if which == 1 and 1 <= t <= n_steps:
                stage1(t - 1, (t - 1) % 2)
            if which == 2 and t >= 2:
                stage2(t - 2, t % 2)


def _sb_kernel(q_ref, k_ref, v_ref, o_ref, mask_ref, tri_ref, zs0, zs1, sp0, sp1, c0, c1, w0, w1,
               carry_ref, acc_ref, *, tq, steps):
    zs_refs, sp_refs, c_refs, w_refs = (zs0, zs1), (sp0, sp1), (c0, c1), (w0, w1)
    row = lax.broadcasted_iota(jnp.int32, (tq, tq), 0)
    col = lax.broadcasted_iota(jnp.int32, (tq, tq), 1)
    causal = jnp.where(col < row, 0.0, NEG)
    mask_ref[:tq, :] = causal
    mask_ref[tq:, :] = causal
    tri_ref[...] = jnp.where(row > col, 1.0, 0.0).astype(BF16)
    masks = _head_masks((tq, LANES))
    carry_ref[...] = jnp.zeros(carry_ref.shape, F32)
    acc_ref[...] = jnp.zeros(acc_ref.shape, F32)

    def logits(n, slot):
        i, kb, diag = steps[n]
        q = q_ref[0, pl.ds(i * tq, tq), :]
        q2 = jnp.concatenate([jnp.where(masks[e], q, jnp.zeros_like(q)) for e in range(2)], axis=0)
        z = _dot_nt(q2, k_ref[0, pl.ds(kb * tq, tq), :])
        if diag:
            z = z + mask_ref[...]
        sp = jnp.maximum(z, 0.0) + jnp.log2(1.0 + jnp.exp2(-jnp.abs(z)))
        zs_refs[slot][...] = z - sp
        sp_refs[slot][...] = sp.astype(BF16)
        carry = carry_ref[i]
        c_refs[slot][...] = carry
        carry_ref[i] = carry + jnp.sum(sp, axis=1, keepdims=True)

    def weights(n, slot):
        right = _dot(sp_refs[slot][...], tri_ref[...])
        right = right + jnp.concatenate([c_refs[slot][...]] * (tq // LANES), axis=1)
        w_refs[slot][...] = jnp.exp2(zs_refs[slot][...] - right).astype(BF16)

    def pv(n, slot):
        i, kb, _ = steps[n]
        acc_ref[i] = acc_ref[i] + _dot(w_refs[slot][...], v_ref[0, pl.ds(kb * tq, tq), :])

    _run_pipeline(len(steps), logits, weights, pv, order=(2, 0, 1))
    for i in range(acc_ref.shape[0]):
        acc = acc_ref[i]
        o_ref[0, pl.ds(i * tq, tq), :] = jnp.where(masks[0], acc[:tq], acc[tq:]).astype(BF16)


def _sb_attention(qkv, *, tq=256):
    batch, seq, _ = qkv.shape
    npair = W_SB // LANES
    steps = _tile_schedule(seq // tq, diag_first=True)
    blk = lambda off: pl.BlockSpec((1, seq, LANES), lambda b, p: (b, 0, off + p))
    zs_shape, sp_shape = pltpu.VMEM((2 * tq, tq), F32), pltpu.VMEM((2 * tq, tq), BF16)
    c_shape, w_shape = pltpu.VMEM((2 * tq, LANES), F32), pltpu.VMEM((2 * tq, tq), BF16)
    state_shape = pltpu.VMEM((seq // tq, 2 * tq, LANES), F32)
    return pl.pallas_call(
        functools.partial(_sb_kernel, tq=tq, steps=steps),
        grid=(batch, npair),
        in_specs=[blk(0), blk(npair), blk(2 * npair)],
        out_specs=pl.BlockSpec((1, seq, LANES), lambda b, p: (b, 0, p)),
        out_shape=jax.ShapeDtypeStruct((batch, seq, W_SB), BF16),
        scratch_shapes=[pltpu.VMEM((2 * tq, tq), F32), pltpu.VMEM((tq, tq), BF16)]
                       + [zs_shape] * 2 + [sp_shape] * 2 + [c_shape] * 2 + [w_shape] * 2 + [state_shape] * 2,
        compiler_params=_params(("parallel", "parallel")),
        name="sb_attn",
    )(qkv, qkv, qkv)


def _fox_kernel(q_ref, k_ref, v_ref, f_ref, o_ref, mask_ref,
                s00, s01, s10, s11, p00, p01, p10, p11, a00, a01, a10, a11,
                m0, m1, acc0, acc1, va0, va1, *, tq, steps):
    s_refs, p_refs, a_refs = ((s00, s01), (s10, s11)), ((p00, p01), (p10, p11)), ((a00, a01), (a10, a11))
    m_refs, acc_refs, va_refs = (m0, m1), (acc0, acc1), (va0, va1)
    row = lax.broadcasted_iota(jnp.int32, (tq, tq), 0)
    col = lax.broadcasted_iota(jnp.int32, (tq, tq), 1)
    mask_ref[...] = jnp.where(col <= row, 0.0, NEG)
    masks = _head_masks((tq, LANES))
    v_all = v_ref[0]
    full = _head_masks(v_all.shape)
    for e in range(2):
        m_refs[e][...] = jnp.full(m_refs[e].shape, NEG, F32)
        acc_refs[e][...] = jnp.zeros(acc_refs[e].shape, F32)
        va_refs[e][...] = jnp.where(full[e], v_all, jnp.ones_like(v_all))

    def qk(n, slot):
        i, kb, _ = steps[n]
        q = q_ref[0, pl.ds(i * tq, tq), :]
        k = k_ref[0, pl.ds(kb * tq, tq), :]
        for e in range(2):
            qm = jnp.where(masks[e], q, jnp.zeros_like(q))
            s_refs[slot][e][...] = _dot_nt(qm, k)

    def sm(n, slot):
        i, kb, diag = steps[n]
        rq, rk = pl.ds(i * tq, tq), pl.ds(kb * tq, tq)
        for e in range(2):
            fq0 = f_ref[e, :, rq][:, 0:1]
            s = s_refs[slot][e][...] + (fq0 - f_ref[e, :, rk])
            if diag:
                s = s + mask_ref[...]
            m_prev = m_refs[e][rq, :]
            m_new = jnp.maximum(m_prev, jnp.max(s, axis=1, keepdims=True))
            m_refs[e][rq, :] = m_new
            a_refs[slot][e][...] = jnp.exp2(m_prev - m_new)
            p_refs[slot][e][...] = jnp.exp2(s - jnp.concatenate([m_new] * (tq // LANES), axis=1)).astype(BF16)

    def pv(n, slot):
        i, kb, _ = steps[n]
        rq, rk = pl.ds(i * tq, tq), pl.ds(kb * tq, tq)
        for e in range(2):
            acc_refs[e][rq, :] = (a_refs[slot][e][...] * acc_refs[e][rq, :]
                                  + _dot(p_refs[slot][e][...], va_refs[e][rk, :]))

    _run_pipeline(len(steps), qk, sm, pv, order=(2, 0, 1))
    outs = []
    for e in range(2):
        acc = acc_refs[e][...]
        outs.append(acc * (1.0 / pltpu.roll(acc, HEAD_DIM, axis=1)))
    o_ref[0] = jnp.where(full[0], outs[0], outs[1]).astype(BF16)


def _fox_attention(qkv, fcum, *, tq=256):
    batch, seq, _ = qkv.shape
    cb = (3 * W_SB + 3 * W_CH) // LANES
    npair = W_FOX // LANES
    steps = _tile_schedule(seq // tq, diag_first=False)
    blk = lambda off: pl.BlockSpec((1, seq, LANES), lambda b, p: (b, 0, cb + off + p))
    s_shape, p_shape, a_shape = pltpu.VMEM((tq, tq), F32), pltpu.VMEM((tq, tq), BF16), pltpu.VMEM((tq, LANES), F32)
    acc_shape, v_shape = pltpu.VMEM((seq, LANES), F32), pltpu.VMEM((seq, LANES), BF16)
    return pl.pallas_call(
        functools.partial(_fox_kernel, tq=tq, steps=steps),
        grid=(batch, npair),
        in_specs=[blk(0), blk(npair), blk(2 * npair),
                  pl.BlockSpec((None, 2, 1, seq), lambda b, p: (b, p, 0, 0))],
        out_specs=pl.BlockSpec((1, seq, LANES), lambda b, p: (b, 0, p)),
        out_shape=jax.ShapeDtypeStruct((batch, seq, W_FOX), BF16),
        scratch_shapes=[pltpu.VMEM((tq, tq), F32)] + [s_shape] * 4 + [p_shape] * 4 + [a_shape] * 4
                       + [acc_shape] * 4 + [v_shape] * 2,
        compiler_params=_params(("parallel", "parallel")),
        name="fox_attn",
    )(qkv, qkv, qkv, fcum)


def _chunk_kernel(q_ref, k_ref, v_ref, r_ref, o_ref, bias_ref, s0_ref, s1_ref, p0_ref, p1_ref, l0_ref, l1_ref,
                  *, seq):
    n_tiles = seq // PAIR_Q
    rows = lax.broadcasted_iota(jnp.int32, (PAIR_Q, PAIR_KEYS), 0)
    cols = lax.broadcasted_iota(jnp.int32, (PAIR_Q, PAIR_KEYS), 1)
    first_key = jnp.where(rows < CHUNK, 0, CHUNK)
    valid = (cols >= first_key) & (cols < first_key + PAIR_KEYS - CHUNK)
    masks = _head_masks((PAIR_Q, LANES))
    for e in range(2):
        gen = jnp.broadcast_to(r_ref[0, e:e + 1, :], (PAIR_Q, REL_VEC))
        toeplitz = pltpu.roll(gen, 0, 1, stride=1, stride_axis=0)
        bias_ref[e * PAIR_Q:(e + 1) * PAIR_Q, :] = jnp.where(valid, toeplitz[:, :PAIR_KEYS], NEG)
    s_refs, p_refs, l_refs = (s0_ref, s1_ref), (p0_ref, p1_ref), (l0_ref, l1_ref)

    def window(t):
        nk = min(PAIR_KEYS, PAIR_Q * (t + 1))
        return pl.ds(PAIR_Q * (t + 1) - nk, nk), nk

    def qk(t, slot):
        keys, nk = window(t)
        q = q_ref[0, pl.ds(t * PAIR_Q, PAIR_Q), :]
        q2 = jnp.concatenate([jnp.where(masks[e], q, jnp.zeros_like(q)) for e in range(2)], axis=0)
        s_refs[slot][:, :nk] = _dot_nt(q2, k_ref[0, keys, :])

    def sm(t, slot):
        _, nk = window(t)
        s = s_refs[slot][:, :nk] + bias_ref[:, PAIR_KEYS - nk:]
        p = jnp.exp2(s - jnp.max(s, axis=1, keepdims=True))
        l_refs[slot][...] = jnp.sum(p, axis=1, keepdims=True)
        p_refs[slot][:, :nk] = p.astype(BF16)

    def pv(t, slot):
        keys, nk = window(t)
        v = v_ref[0, keys, :]
        outs = []
        for e in range(2):
            rows = pl.ds(e * PAIR_Q, PAIR_Q)
            outs.append(_dot(p_refs[slot][rows, :nk], v) * (1.0 / l_refs[slot][rows, :]))
        o_ref[0, pl.ds(t * PAIR_Q, PAIR_Q), :] = jnp.where(masks[0], outs[0], outs[1]).astype(BF16)

    _run_pipeline(n_tiles, qk, sm, pv)


def _chunk_attention(qkv, rel_gen):
    batch, seq, _ = qkv.shape
    cb = 3 * W_SB // LANES
    npair = W_CH // LANES
    blk = lambda off: pl.BlockSpec((1, seq, LANES), lambda b, p: (b, 0, cb + off + p))
    s_shape = pltpu.VMEM((2 * PAIR_Q, PAIR_KEYS), F32)
    p_shape = pltpu.VMEM((2 * PAIR_Q, PAIR_KEYS), BF16)
    l_shape = pltpu.VMEM((2 * PAIR_Q, 1), F32)
    return pl.pallas_call(
        functools.partial(_chunk_kernel, seq=seq),
        grid=(batch, npair),
        in_specs=[blk(0), blk(npair), blk(2 * npair),
                  pl.BlockSpec((1, 2, REL_VEC), lambda b, p: (p, 0, 0))],
        out_specs=pl.BlockSpec((1, seq, LANES), lambda b, p: (b, 0, p)),
        out_shape=jax.ShapeDtypeStruct((batch, seq, W_CH), BF16),
        scratch_shapes=[s_shape, s_shape, s_shape, p_shape, p_shape, l_shape, l_shape],
        compiler_params=_params(("parallel", "parallel")),
        name="chunk_attn",
    )(qkv, qkv, qkv, rel_gen)


def _merge_kernel(x_ref, oa_ref, ob_ref, oc_ref, g_ref, wg_ref, bg_ref, wa_ref, wb_ref, wc_ref, wo_ref, o_ref):
    x = x_ref[...]
    h = _rms_bf16(x, g_ref[...])
    gates = jax.nn.sigmoid(_dot(h, wg_ref[...]) + bg_ref[...])
    merged = (gates[:, :D_MODEL] * _dot(oa_ref[...], wa_ref[...])
              + gates[:, D_MODEL:2 * D_MODEL] * _dot(ob_ref[...], wb_ref[...])
              + gates[:, 2 * D_MODEL:] * _dot(oc_ref[...], wc_ref[...]))
    o_ref[...] = x + _dot(merged.astype(BF16), wo_ref[...])


def _merge(x, oa, ob, oc, g, wg, bg, wa, wb, wc, wo, *, layer, tm=512):
    n, d = x.shape
    rows = lambda w: pl.BlockSpec((tm, w), lambda i: (i, 0))
    consts = [g, wg, bg, wa, wb, wc, wo]
    return pl.pallas_call(
        _merge_kernel,
        grid=(n // tm,),
        in_specs=[rows(d), rows(oa.shape[1]), rows(ob.shape[1]), rows(oc.shape[1])]
                 + [_layer_spec(c, layer) if c.ndim == 3 else _const_spec(c.shape) for c in consts],
        out_specs=rows(d),
        out_shape=jax.ShapeDtypeStruct((n, d), F32),
        compiler_params=_params(("parallel",)),
        name="merge",
    )(x, oa, ob, oc, *consts)


def _rel_generator_index():
    j = np.arange(REL_VEC)
    dist = np.where(j <= PAIR_KEYS, LEFT_CHUNKS * CHUNK - j, MAX_REL)
    return np.clip(dist, -MAX_REL, MAX_REL) + MAX_REL


def _cast_kernel(w_ref, o_ref):
    o_ref[...] = w_ref[...].astype(BF16)


def _cast_bf16(w, rows):
    l, r, c = w.shape
    blk = pl.BlockSpec((1, rows, c), lambda li, ri: (li, ri, 0))
    return pl.pallas_call(
        _cast_kernel, grid=(l, r // rows), in_specs=[blk], out_specs=blk,
        out_shape=jax.ShapeDtypeStruct(w.shape, BF16),
        compiler_params=_params(("parallel", "parallel")), name="cast",
    )(w)


def _split_w_in_kernel(wq_ref, wg_ref, wf_ref, scale_ref, qkv_ref, g_ref, f_ref):
    nf = N_HEADS_FOX
    for l in range(qkv_ref.shape[0]):
        qkv_ref[l] = (wq_ref[:, l, :] * scale_ref[...]).T.astype(BF16)
        g_ref[l] = wg_ref[:, l, :].T.astype(BF16)
        slab = wf_ref[:, l, :].T
        lane = lax.broadcasted_iota(jnp.int32, slab.shape, 1)
        f_ref[l] = jnp.where(lane < nf, slab, 0.0).astype(BF16)


def _split_w_in(w_in_t, col_scale_t, cols=512):
    width, l, d = w_in_t.shape
    ng = width - QKV_WIDTH - N_HEADS_FOX
    src = lambda n, start: pl.BlockSpec((pl.Element(n), pl.Element(l), pl.Element(d)),
                                        lambda j: (start + j * n, 0, 0))
    dst = lambda c: pl.BlockSpec((l, d, c), lambda j: (0, 0, j))
    return pl.pallas_call(
        _split_w_in_kernel, grid=(QKV_WIDTH // cols,),
        in_specs=[src(cols, 0), src(cols, QKV_WIDTH + N_HEADS_FOX),
                  pl.BlockSpec((pl.Element(LANES), pl.Element(l), pl.Element(d)), lambda j: (QKV_WIDTH, 0, 0)),
                  pl.BlockSpec((cols, 1), lambda j: (j, 0))],
        out_specs=[dst(cols), dst(cols), pl.BlockSpec((l, d, LANES), lambda j: (0, 0, 0))],
        out_shape=[jax.ShapeDtypeStruct((l, d, QKV_WIDTH), BF16), jax.ShapeDtypeStruct((l, d, ng), BF16),
                   jax.ShapeDtypeStruct((l, d, LANES), BF16)],
        compiler_params=_params(("arbitrary",)), name="split_w_in",
    )(w_in_t, w_in_t, w_in_t, col_scale_t)


def _q_col_scale(width):
    col_scale = np.ones((1, width), np.float32)
    for start, w in ((0, W_SB), (3 * W_SB, W_CH), (3 * W_SB + 3 * W_CH, W_FOX)):
        col_scale[:, start:start + w] = HEAD_DIM ** -0.5 * LOG2E
    return jnp.asarray(col_scale)


def kernel(x, g_ffn1, w_ffn1_in, w_ffn1_out, g_mix, w_in, b_in, rel_bias, w_br_sb, w_br_ch, w_br_fox, w_out,
           g_ffn2, w_ffn2_in, w_ffn2_out, g_final):
    batch, seq, d = x.shape
    depth = g_ffn1.shape[0]
    nf = N_HEADS_FOX
    col_scale = _q_col_scale(w_in.shape[-1])
    ffn1_in, ffn2_in = _cast_bf16(w_ffn1_in, 256), _cast_bf16(w_ffn2_in, 256)
    ffn1_out, ffn2_out = _cast_bf16(w_ffn1_out, 704), _cast_bf16(w_ffn2_out, 704)
    br_sb, br_ch, br_fox = _cast_bf16(w_br_sb, W_SB), _cast_bf16(w_br_ch, W_CH), _cast_bf16(w_br_fox, W_FOX)
    wo = _cast_bf16(w_out, d)
    w_qkv, w_g, w_f = _split_w_in(jnp.transpose(w_in, (2, 0, 1)), col_scale[0, :QKV_WIDTH, None])
    b_s = b_in * col_scale
    b_qkv = b_s[:, :QKV_WIDTH]
    b_f = jnp.pad(b_s[:, QKV_WIDTH:QKV_WIDTH + nf], ((0, 0), (0, LANES - nf)))
    b_g = b_s[:, QKV_WIDTH + nf:]
    rel_gen = (rel_bias[:, _rel_generator_index(), :] * LOG2E).transpose(0, 2, 1).reshape(
        depth, N_HEADS_CH // 2, 2, REL_VEC)
    xf = x.reshape(batch * seq, d)
    gf = g_final[None, :]
    for l in range(depth):
        xf = _ffn(xf, g_ffn1[l][None, :], ffn1_in, ffn1_out, gf, layer=l, final_norm=False)
        qkv, fcum = _inproj(xf, g_mix[l][None, :], w_qkv, b_qkv[l][None, :], w_f, b_f[l][None, :],
                            layer=l, batch=batch)
        qkv = qkv.reshape(batch, seq, QKV_WIDTH)
        o_a = _sb_attention(qkv)
        o_b = _chunk_attention(qkv, rel_gen[l])
        o_c = _fox_attention(qkv, fcum[:, :nf].reshape(batch, nf, 1, seq))
        xf = _merge(xf, o_a.reshape(batch * seq, W_SB), o_b.reshape(batch * seq, W_CH),
                    o_c.reshape(batch * seq, W_FOX), g_mix[l][None, :], w_g, b_g[l][None, :],
                    br_sb, br_ch, br_fox, wo, layer=l)
        xf = _ffn(xf, g_ffn2[l][None, :], ffn2_in, ffn2_out, gf, layer=l, final_norm=(l == depth - 1))
    return xf.reshape(batch, seq, d)
```

```python
import functools

import numpy as np
import jax
import jax.numpy as jnp
from jax import lax
from jax.experimental import pallas as pl
from jax.experimental.pallas import tpu as pltpu

D_MODEL = 1024
HEAD_DIM = 64
CHUNK = 64
LEFT_CHUNKS = 8
MAX_REL = 128
N_HEADS_SB = 4
N_HEADS_CH = 8
N_HEADS_FOX = 4
W_SB = N_HEADS_SB * HEAD_DIM
W_CH = N_HEADS_CH * HEAD_DIM
W_FOX = N_HEADS_FOX * HEAD_DIM
QKV_WIDTH = 3 * (W_SB + W_CH + W_FOX)
RMS_EPS = 1e-6
NEG = -1e30
LOG2E = 1.4426950408889634

LANES = 128
PAIR_Q = 2 * CHUNK
PAIR_KEYS = PAIR_Q + LEFT_CHUNKS * CHUNK
REL_VEC = 768
VMEM_LIMIT = 56 * 1024 * 1024

F32 = jnp.float32
BF16 = jnp.bfloat16


def _rms_bf16(x, g):
    ms = jnp.mean(x * x, axis=-1, keepdims=True)
    return ((x * lax.rsqrt(ms + RMS_EPS)) * g).astype(BF16)


def _dot(a, b):
    return jnp.dot(a, b, preferred_element_type=F32)


def _dot_nt(a, b):
    return lax.dot_general(a, b, (((1,), (1,)), ((), ())), preferred_element_type=F32)


def _const_spec(shape):
    nd = len(shape)
    return pl.BlockSpec(shape, lambda *_: (0,) * nd, pipeline_mode=pl.Buffered(1))


def _layer_spec(w, layer, col_blocks=1, col=0):
    _, r, c = w.shape
    return pl.BlockSpec((None, r, c // col_blocks), lambda *_: (layer, 0, col), pipeline_mode=pl.Buffered(1))


def _params(sem):
    return pltpu.CompilerParams(dimension_semantics=sem, vmem_limit_bytes=VMEM_LIMIT)


def _ffn_kernel(x_ref, g_ref, wg_ref, wu_ref, wo_ref, gf_ref, o_ref, *, final_norm):
    x = x_ref[...]
    h = _rms_bf16(x, g_ref[...])
    gate = _dot(h, wg_ref[...])
    up = _dot(h, wu_ref[...])
    a = (gate * jax.nn.sigmoid(gate) * up).astype(BF16)
    y = x + 0.5 * _dot(a, wo_ref[...])
    if final_norm:
        ms = jnp.mean(y * y, axis=-1, keepdims=True)
        y = (y * lax.rsqrt(ms + RMS_EPS)) * gf_ref[...]
    o_ref[...] = y


def _ffn(x, g, w_in, w_out, gf, *, layer, final_norm, tm=512):
    n, d = x.shape
    row = pl.BlockSpec((tm, d), lambda i: (i, 0))
    return pl.pallas_call(
        functools.partial(_ffn_kernel, final_norm=final_norm),
        grid=(n // tm,),
        in_specs=[row, _const_spec(g.shape), _layer_spec(w_in, layer, 2, 0), _layer_spec(w_in, layer, 2, 1),
                  _layer_spec(w_out, layer), _const_spec(gf.shape)],
        out_specs=row,
        out_shape=jax.ShapeDtypeStruct((n, d), F32),
        compiler_params=_params(("parallel",)),
        name="ffn",
    )(x, g, w_in, w_in, w_out, gf)


def _inproj_kernel(x_ref, g_ref, w_ref, b_ref, wf_ref, bf_ref, qkv_ref, fc_ref, carry_ref, *, tm):
    @pl.when(pl.program_id(1) == 0)
    def _():
        carry_ref[...] = jnp.zeros_like(carry_ref)

    h = _rms_bf16(x_ref[...], g_ref[...])
    f = _dot(h, wf_ref[...]) + bf_ref[...]
    lf = (jnp.minimum(f, 0.0) - jnp.log1p(jnp.exp(-jnp.abs(f)))) * LOG2E
    s = lf.T[:8, :]
    lane = lax.broadcasted_iota(jnp.int32, s.shape, 1)
    sh = 1
    while sh < tm:
        s = s + jnp.where(lane >= sh, pltpu.roll(s, sh, axis=1), 0.0)
        sh *= 2
    fc = s + carry_ref[:, 0:1]
    fc_ref[0] = fc
    carry_ref[...] = jnp.broadcast_to(fc[:, tm - 1:tm], carry_ref.shape)
    qkv_ref[...] = (_dot(h, w_ref[...]) + b_ref[...]).astype(BF16)


def _inproj(x, g, w, b, wf, bfg, *, layer, batch, tm=512):
    n, d = x.shape
    seq = n // batch
    nt = seq // tm
    return pl.pallas_call(
        functools.partial(_inproj_kernel, tm=tm),
        grid=(batch, nt),
        in_specs=[pl.BlockSpec((tm, d), lambda bi, ti: (bi * nt + ti, 0)),
                  _const_spec(g.shape), _layer_spec(w, layer), _const_spec(b.shape),
                  _layer_spec(wf, layer), _const_spec(bfg.shape)],
        out_specs=[pl.BlockSpec((tm, QKV_WIDTH), lambda bi, ti: (bi * nt + ti, 0)),
                   pl.BlockSpec((1, 8, tm), lambda bi, ti: (bi, 0, ti))],
        out_shape=[jax.ShapeDtypeStruct((n, QKV_WIDTH), BF16),
                   jax.ShapeDtypeStruct((batch, 8, seq), F32)],
        scratch_shapes=[pltpu.VMEM((8, LANES), F32)],
        compiler_params=_params(("parallel", "arbitrary")),
        name="inproj",
    )(x, g, w, b, wf, bfg)


def _head_masks(shape):
    lane = lax.broadcasted_iota(jnp.int32, shape, 1)
    lo = lane < HEAD_DIM
    return lo, jnp.logical_not(lo)


def _tile_schedule(n_tiles, diag_first):
    steps = []
    for i in range(n_tiles):
        kbs = range(i, -1, -1) if diag_first else range(i + 1)
        steps += [(i, kb, kb == i) for kb in kbs]
    return steps


def _run_pipeline(n_steps, stage0, stage1, stage2, order=(0, 1, 2)):
    for t in range(n_steps + 2):
        for which in order:
            if which == 0 and t < n_steps:
                stage0(t, t % 2)
            if which == 1 and 1 <= t <= n_steps:
                stage1(t - 1, (t - 1) % 2)
            if which == 2 and t >= 2:
                stage2(t - 2, t % 2)


def _sb_kernel(q_ref, k_ref, v_ref, o_ref, mask_ref, tri_ref, zs0, zs1, sp0, sp1, c0, c1, w0, w1,
               carry_ref, acc_ref, *, tq, steps):
    zs_refs, sp_refs, c_refs, w_refs = (zs0, zs1), (sp0, sp1), (c0, c1), (w0, w1)
    row = lax.broadcasted_iota(jnp.int32, (tq, tq), 0)
    col = lax.broadcasted_iota(jnp.int32, (tq, tq), 1)
    causal = jnp.where(col < row, 0.0, NEG)
    mask_ref[:tq, :] = causal
    mask_ref[tq:, :] = causal
    tri_ref[...] = jnp.where(row > col, 1.0, 0.0).astype(BF16)
    masks = _head_masks((tq, LANES))
    carry_ref[...] = jnp.zeros(carry_ref.shape, F32)
    acc_ref[...] = jnp.zeros(acc_ref.shape, F32)

    def logits(n, slot):
        i, kb, diag = steps[n]
        q = q_ref[0, pl.ds(i * tq, tq), :]
        q2 = jnp.concatenate([jnp.where(masks[e], q, jnp.zeros_like(q)) for e in range(2)], axis=0)
        z = _dot_nt(q2, k_ref[0, pl.ds(kb * tq, tq), :])
        if diag:
            z = z + mask_ref[...]
        sp = jnp.maximum(z, 0.0) + jnp.log2(1.0 + jnp.exp2(-jnp.abs(z)))
        zs_refs[slot][...] = z - sp
        sp_refs[slot][...] = sp.astype(BF16)
        carry = carry_ref[i]
        c_refs[slot][...] = carry
        carry_ref[i] = carry + jnp.sum(sp, axis=1, keepdims=True)

    def weights(n, slot):
        right = _dot(sp_refs[slot][...], tri_ref[...])
        right = right + jnp.concatenate([c_refs[slot][...]] * (tq // LANES), axis=1)
        w_refs[slot][...] = jnp.exp2(zs_refs[slot][...] - right).astype(BF16)

    def pv(n, slot):
        i, kb, _ = steps[n]
        acc_ref[i] = acc_ref[i] + _dot(w_refs[slot][...], v_ref[0, pl.ds(kb * tq, tq), :])

    _run_pipeline(len(steps), logits, weights, pv, order=(2, 0, 1))
    for i in range(acc_ref.shape[0]):
        acc = acc_ref[i]
        o_ref[0, pl.ds(i * tq, tq), :] = jnp.where(masks[0], acc[:tq], acc[tq:]).astype(BF16)


def _sb_attention(qkv, *, tq=256):
    batch, seq, _ = qkv.shape
    npair = W_SB // LANES
    steps = _tile_schedule(seq // tq, diag_first=True)
    blk = lambda off: pl.BlockSpec((1, seq, LANES), lambda b, p: (b, 0, off + p))
    zs_shape, sp_shape = pltpu.VMEM((2 * tq, tq), F32), pltpu.VMEM((2 * tq, tq), BF16)
    c_shape, w_shape = pltpu.VMEM((2 * tq, LANES), F32), pltpu.VMEM((2 * tq, tq), BF16)
    state_shape = pltpu.VMEM((seq // tq, 2 * tq, LANES), F32)
    return pl.pallas_call(
        functools.partial(_sb_kernel, tq=tq, steps=steps),
        grid=(batch, npair),
        in_specs=[blk(0), blk(npair), blk(2 * npair)],
        out_specs=pl.BlockSpec((1, seq, LANES), lambda b, p: (b, 0, p)),
        out_shape=jax.ShapeDtypeStruct((batch, seq, W_SB), BF16),
        scratch_shapes=[pltpu.VMEM((2 * tq, tq), F32), pltpu.VMEM((tq, tq), BF16)]
                       + [zs_shape] * 2 + [sp_shape] * 2 + [c_shape] * 2 + [w_shape] * 2 + [state_shape] * 2,
        compiler_params=_params(("parallel", "parallel")),
        name="sb_attn",
    )(qkv, qkv, qkv)


def _fox_kernel(q_ref, k_ref, v_ref, f_ref, o_ref, mask_ref,
                s00, s01, s10, s11, p00, p01, p10, p11, a00, a01, a10, a11,
                m0, m1, acc0, acc1, va0, va1, *, tq, steps):
    s_refs, p_refs, a_refs = ((s00, s01), (s10, s11)), ((p00, p01), (p10, p11)), ((a00, a01), (a10, a11))
    m_refs, acc_refs, va_refs = (m0, m1), (acc0, acc1), (va0, va1)
    row = lax.broadcasted_iota(jnp.int32, (tq, tq), 0)
    col = lax.broadcasted_iota(jnp.int32, (tq, tq), 1)
    mask_ref[...] = jnp.where(col <= row, 0.0, NEG)
    masks = _head_masks((tq, LANES))
    v_all = v_ref[0]
    full = _head_masks(v_all.shape)
    for e in range(2):
        m_refs[e][...] = jnp.full(m_refs[e].shape, NEG, F32)
        acc_refs[e][...] = jnp.zeros(acc_refs[e].shape, F32)
        va_refs[e][...] = jnp.where(full[e], v_all, jnp.ones_like(v_all))

    def qk(n, slot):
        i, kb, _ = steps[n]
        q = q_ref[0, pl.ds(i * tq, tq), :]
        k = k_ref[0, pl.ds(kb * tq, tq), :]
        for e in range(2):
            qm = jnp.where(masks[e], q, jnp.zeros_like(q))
            s_refs[slot][e][...] = _dot_nt(qm, k)

    def sm(n, slot):
        i, kb, diag = steps[n]
        rq, rk = pl.ds(i * tq, tq), pl.ds(kb * tq, tq)
        for e in range(2):
            fq0 = f_ref[e, :, rq][:, 0:1]
            s = s_refs[slot][e][...] + (fq0 - f_ref[e, :, rk])
            if diag:
                s = s + mask_ref[...]
            m_prev = m_refs[e][rq, :]
            m_new = jnp.maximum(m_prev, jnp.max(s, axis=1, keepdims=True))
            m_refs[e][rq, :] = m_new
            a_refs[slot][e][...] = jnp.exp2(m_prev - m_new)
            p_refs[slot][e][...] = jnp.exp2(s - jnp.concatenate([m_new] * (tq // LANES), axis=1)).astype(BF16)

    def pv(n, slot):
        i, kb, _ = steps[n]
        rq, rk = pl.ds(i * tq, tq), pl.ds(kb * tq, tq)
        for e in range(2):
            acc_refs[e][rq, :] = (a_refs[slot][e][...] * acc_refs[e][rq, :]
                                  + _dot(p_refs[slot][e][...], va_refs[e][rk, :]))

    _run_pipeline(len(steps), qk, sm, pv, order=(2, 0, 1))
    outs = []
    for e in range(2):
        acc = acc_refs[e][...]
        outs.append(acc * (1.0 / pltpu.roll(acc, HEAD_DIM, axis=1)))
    o_ref[0] = jnp.where(full[0], outs[0], outs[1]).astype(BF16)


def _fox_attention(qkv, fcum, *, tq=256):
    batch, seq, _ = qkv.shape
    cb = (3 * W_SB + 3 * W_CH) // LANES
    npair = W_FOX // LANES
    steps = _tile_schedule(seq // tq, diag_first=False)
    blk = lambda off: pl.BlockSpec((1, seq, LANES), lambda b, p: (b, 0, cb + off + p))
    s_shape, p_shape, a_shape = pltpu.VMEM((tq, tq), F32), pltpu.VMEM((tq, tq), BF16), pltpu.VMEM((tq, LANES), F32)
    acc_shape, v_shape = pltpu.VMEM((seq, LANES), F32), pltpu.VMEM((seq, LANES), BF16)
    return pl.pallas_call(
        functools.partial(_fox_kernel, tq=tq, steps=steps),
        grid=(batch, npair),
        in_specs=[blk(0), blk(npair), blk(2 * npair),
                  pl.BlockSpec((None, 2, 1, seq), lambda b, p: (b, p, 0, 0))],
        out_specs=pl.BlockSpec((1, seq, LANES), lambda b, p: (b, 0, p)),
        out_shape=jax.ShapeDtypeStruct((batch, seq, W_FOX), BF16),
        scratch_shapes=[pltpu.VMEM((tq, tq), F32)] + [s_shape] * 4 + [p_shape] * 4 + [a_shape] * 4
                       + [acc_shape] * 4 + [v_shape] * 2,
        compiler_params=_params(("parallel", "parallel")),
        name="fox_attn",
    )(qkv, qkv, qkv, fcum)


def _chunk_kernel(q_ref, k_ref, v_ref, r_ref, o_ref, bias_ref, s0_ref, s1_ref, p0_ref, p1_ref, l0_ref, l1_ref,
                  *, seq, pairs):
    n_tiles = seq // PAIR_Q
    rows = lax.broadcasted_iota(jnp.int32, (PAIR_Q, PAIR_KEYS), 0)
    cols = lax.broadcasted_iota(jnp.int32, (PAIR_Q, PAIR_KEYS), 1)
    first_key = jnp.where(rows < CHUNK, 0, CHUNK)
    valid = (cols >= first_key) & (cols < first_key + PAIR_KEYS - CHUNK)
    masks = _head_masks((PAIR_Q, LANES))
    for h in range(2 * pairs):
        gen = jnp.broadcast_to(r_ref[h // 2, h % 2:h % 2 + 1, :], (PAIR_Q, REL_VEC))
        toeplitz = pltpu.roll(gen, 0, 1, stride=1, stride_axis=0)
        bias_ref[h * PAIR_Q:(h + 1) * PAIR_Q, :] = jnp.where(valid, toeplitz[:, :PAIR_KEYS], NEG)
    s_refs, p_refs, l_refs = (s0_ref, s1_ref), (p0_ref, p1_ref), (l0_ref, l1_ref)
    steps = [(pp, t) for pp in range(pairs) for t in range(n_tiles)]

    def window(t):
        nk = min(PAIR_KEYS, PAIR_Q * (t + 1))
        return pl.ds(PAIR_Q * (t + 1) - nk, nk), nk

    def qk(n, slot):
        pp, t = steps[n]
        keys, nk = window(t)
        lanes = pl.ds(pp * LANES, LANES)
        q = q_ref[0, pl.ds(t * PAIR_Q, PAIR_Q), lanes]
        q2 = jnp.concatenate([jnp.where(masks[e], q, jnp.zeros_like(q)) for e in range(2)], axis=0)
        s_refs[slot][:, :nk] = _dot_nt(q2, k_ref[0, keys, lanes])

    def sm(n, slot):
        pp, t = steps[n]
        _, nk = window(t)
        s = s_refs[slot][:, :nk] + bias_ref[pl.ds(pp * 2 * PAIR_Q, 2 * PAIR_Q), PAIR_KEYS - nk:]
        p = jnp.exp2(s - jnp.max(s, axis=1, keepdims=True))
        l_refs[slot][...] = jnp.sum(p, axis=1, keepdims=True)
        p_refs[slot][:, :nk] = p.astype(BF16)

    def pv(n, slot):
        pp, t = steps[n]
        keys, nk = window(t)
        lanes = pl.ds(pp * LANES, LANES)
        v = v_ref[0, keys, lanes]
        outs = []
        for e in range(2):
            rows_e = pl.ds(e * PAIR_Q, PAIR_Q)
            outs.append(_dot(p_refs[slot][rows_e, :nk], v) * (1.0 / l_refs[slot][rows_e, :]))
        o_ref[0, pl.ds(t * PAIR_Q, PAIR_Q), lanes] = jnp.where(masks[0], outs[0], outs[1]).astype(BF16)

    _run_pipeline(len(steps), qk, sm, pv)


def _chunk_attention(qkv, rel_gen, *, pairs=2):
    batch, seq, _ = qkv.shape
    width = pairs * LANES
    cb = 3 * W_SB // width
    ngrp = W_CH // width
    blk = lambda off: pl.BlockSpec((1, seq, width), lambda b, g: (b, 0, cb + off + g))
    s_shape = pltpu.VMEM((2 * PAIR_Q, PAIR_KEYS), F32)
    p_shape = pltpu.VMEM((2 * PAIR_Q, PAIR_KEYS), BF16)
    l_shape = pltpu.VMEM((2 * PAIR_Q, 1), F32)
    return pl.pallas_call(
        functools.partial(_chunk_kernel, seq=seq, pairs=pairs),
        grid=(batch, ngrp),
        in_specs=[blk(0), blk(ngrp), blk(2 * ngrp),
                  pl.BlockSpec((pairs, 2, REL_VEC), lambda b, g: (g, 0, 0))],
        out_specs=pl.BlockSpec((1, seq, width), lambda b, g: (b, 0, g)),
        out_shape=jax.ShapeDtypeStruct((batch, seq, W_CH), BF16),
        scratch_shapes=[pltpu.VMEM((2 * pairs * PAIR_Q, PAIR_KEYS), F32),
                        s_shape, s_shape, p_shape, p_shape, l_shape, l_shape],
        compiler_params=_params(("parallel", "parallel")),
        name="chunk_attn",
    )(qkv, qkv, qkv, rel_gen)


def _merge_kernel(x_ref, oa_ref, ob_ref, oc_ref, g_ref, wg_ref, bg_ref, wa_ref, wb_ref, wc_ref, wo_ref, o_ref):
    x = x_ref[...]
    h = _rms_bf16(x, g_ref[...])
    gates = jax.nn.sigmoid(_dot(h, wg_ref[...]) + bg_ref[...])
    merged = (gates[:, :D_MODEL] * _dot(oa_ref[...], wa_ref[...])
              + gates[:, D_MODEL:2 * D_MODEL] * _dot(ob_ref[...], wb_ref[...])
              + gates[:, 2 * D_MODEL:] * _dot(oc_ref[...], wc_ref[...]))
    o_ref[...] = x + _dot(merged.astype(BF16), wo_ref[...])


def _merge(x, oa, ob, oc, g, wg, bg, wa, wb, wc, wo, *, layer, tm=512):
    n, d = x.shape
    rows = lambda w: pl.BlockSpec((tm, w), lambda i: (i, 0))
    consts = [g, wg, bg, wa, wb, wc, wo]
    return pl.pallas_call(
        _merge_kernel,
        grid=(n // tm,),
        in_specs=[rows(d), rows(oa.shape[1]), rows(ob.shape[1]), rows(oc.shape[1])]
                 + [_layer_spec(c, layer) if c.ndim == 3 else _const_spec(c.shape) for c in consts],
        out_specs=rows(d),
        out_shape=jax.ShapeDtypeStruct((n, d), F32),
        compiler_params=_params(("parallel",)),
        name="merge",
    )(x, oa, ob, oc, *consts)


def _rel_generator_index():
    j = np.arange(REL_VEC)
    dist = np.where(j <= PAIR_KEYS, LEFT_CHUNKS * CHUNK - j, MAX_REL)
    return np.clip(dist, -MAX_REL, MAX_REL) + MAX_REL


def _cast_kernel(w_ref, o_ref):
    o_ref[...] = w_ref[...].astype(BF16)


def _cast_bf16(w, rows):
    l, r, c = w.shape
    blk = pl.BlockSpec((1, rows, c), lambda li, ri: (li, ri, 0))
    return pl.pallas_call(
        _cast_kernel, grid=(l, r // rows), in_specs=[blk], out_specs=blk,
        out_shape=jax.ShapeDtypeStruct(w.shape, BF16),
        compiler_params=_params(("parallel", "parallel")), name="cast",
    )(w)


def _split_w_in_kernel(wq_ref, wg_ref, wf_ref, scale_ref, qkv_ref, g_ref, f_ref):
    nf = N_HEADS_FOX
    for l in range(qkv_ref.shape[0]):
        qkv_ref[l] = (wq_ref[:, l, :] * scale_ref[...]).T.astype(BF16)
        g_ref[l] = wg_ref[:, l, :].T.astype(BF16)
        slab = wf_ref[:, l, :].T
        lane = lax.broadcasted_iota(jnp.int32, slab.shape, 1)
        f_ref[l] = jnp.where(lane < nf, slab, 0.0).astype(BF16)


def _split_w_in(w_in_t, col_scale_t, cols=512):
    width, l, d = w_in_t.shape
    ng = width - QKV_WIDTH - N_HEADS_FOX
    src = lambda n, start: pl.BlockSpec((pl.Element(n), pl.Element(l), pl.Element(d)),
                                        lambda j: (start + j * n, 0, 0))
    dst = lambda c: pl.BlockSpec((l, d, c), lambda j: (0, 0, j))
    return pl.pallas_call(
        _split_w_in_kernel, grid=(QKV_WIDTH // cols,),
        in_specs=[src(cols, 0), src(cols, QKV_WIDTH + N_HEADS_FOX),
                  pl.BlockSpec((pl.Element(LANES), pl.Element(l), pl.Element(d)), lambda j: (QKV_WIDTH, 0, 0)),
                  pl.BlockSpec((cols, 1), lambda j: (j, 0))],
        out_specs=[dst(cols), dst(cols), pl.BlockSpec((l, d, LANES), lambda j: (0, 0, 0))],
        out_shape=[jax.ShapeDtypeStruct((l, d, QKV_WIDTH), BF16), jax.ShapeDtypeStruct((l, d, ng), BF16),
                   jax.ShapeDtypeStruct((l, d, LANES), BF16)],
        compiler_params=_params(("arbitrary",)), name="split_w_in",
    )(w_in_t, w_in_t, w_in_t, col_scale_t)


def _q_col_scale(width):
    col_scale = np.ones((1, width), np.float32)
    for start, w in ((0, W_SB), (3 * W_SB, W_CH), (3 * W_SB + 3 * W_CH, W_FOX)):
        col_scale[:, start:start + w] = HEAD_DIM ** -0.5 * LOG2E
    return jnp.asarray(col_scale)


def kernel(x, g_ffn1, w_ffn1_in, w_ffn1_out, g_mix, w_in, b_in, rel_bias, w_br_sb, w_br_ch, w_br_fox, w_out,
           g_ffn2, w_ffn2_in, w_ffn2_out, g_final):
    batch, seq, d = x.shape
    depth = g_ffn1.shape[0]
    nf = N_HEADS_FOX
    col_scale = _q_col_scale(w_in.shape[-1])
    ffn1_in, ffn2_in = _cast_bf16(w_ffn1_in, 256), _cast_bf16(w_ffn2_in, 256)
    ffn1_out, ffn2_out = _cast_bf16(w_ffn1_out, 704), _cast_bf16(w_ffn2_out, 704)
    br_sb, br_ch, br_fox = _cast_bf16(w_br_sb, W_SB), _cast_bf16(w_br_ch, W_CH), _cast_bf16(w_br_fox, W_FOX)
    wo = _cast_bf16(w_out, d)
    w_qkv, w_g, w_f = _split_w_in(jnp.transpose(w_in, (2, 0, 1)), col_scale[0, :QKV_WIDTH, None])
    b_s = b_in * col_scale
    b_qkv = b_s[:, :QKV_WIDTH]
    b_f = jnp.pad(b_s[:, QKV_WIDTH:QKV_WIDTH + nf], ((0, 0), (0, LANES - nf)))
    b_g = b_s[:, QKV_WIDTH + nf:]
    rel_gen = (rel_bias[:, _rel_generator_index(), :] * LOG2E).transpose(0, 2, 1).reshape(
        depth, N_HEADS_CH // 2, 2, REL_VEC)
    xf = x.reshape(batch * seq, d)
    gf = g_final[None, :]
    for l in range(depth):
        xf = _ffn(xf, g_ffn1[l][None, :], ffn1_in, ffn1_out, gf, layer=l, final_norm=False)
        qkv, fcum = _inproj(xf, g_mix[l][None, :], w_qkv, b_qkv[l][None, :], w_f, b_f[l][None, :],
                            layer=l, batch=batch)
        qkv = qkv.reshape(batch, seq, QKV_WIDTH)
        o_a = _sb_attention(qkv)
        o_b = _chunk_attention(qkv, rel_gen[l])
        o_c = _fox_attention(qkv, fcum[:, :nf].reshape(batch, nf, 1, seq))
        xf = _merge(xf, o_a.reshape(batch * seq, W_SB), o_b.reshape(batch * seq, W_CH),
                    o_c.reshape(batch * seq, W_FOX), g_mix[l][None, :], w_g, b_g[l][None, :],
                    br_sb, br_ch, br_fox, wo, layer=l)
        xf = _ffn(xf, g_ffn2[l][None, :], ffn2_in, ffn2_out, gf, layer=l, final_norm=(l == depth - 1))
    return xf.reshape(batch, seq, d)
```
